```python
import jax, jax.numpy as jnp
from jax import lax
import numpy as np

D_MODEL = 1024
BATCH = 16
SEQ = 4096
DEPTH = 1

CTX_LEN = 256
GRID_W = 64
F32 = jnp.float32
EPS = 1e-6
N_MOD = 6
D_MIX = D_MODEL
D_LRU = D_MIX // 2
LRU_BLOCKS = 8
LRU_BLOCK_DIM = D_LRU // LRU_BLOCKS
CONV_W = 4
LRU_C = 8.0
D_RET = D_MIX - D_LRU
RET_HEADS = 4
RET_HEAD_DIM = D_RET // RET_HEADS
RET_CHUNK = 128
ROPE_BASE = 10000.0
N_EXPERTS = 64
TOP_K = 6
N_GROUPS = 8
TOP_GROUPS = 4
D_EXPERT = D_MODEL // 4
D_SHARED = D_EXPERT
ROUTE_SCALE = 2.5
MOE_BLOCK = 128

kernel_name = "hybrid_lru_retention_moe_dit_block"


def rmsnorm(x, g):
    x32 = x.astype(F32)
    y = x32 * lax.rsqrt(jnp.mean(x32 * x32, axis=-1, keepdims=True) + EPS)
    return (y * g.astype(F32)).astype(x.dtype)


def modulate(h, shift, scale):
    return h * (1.0 + scale) + shift


def swiglu(x, w_gate, w_up, w_down):
    return (jax.nn.silu(x @ w_gate) * (x @ w_up)) @ w_down


def dwconv(u, w, b):
    T = u.shape[1]
    left = CONV_W // 2
    up = jnp.pad(u, ((0, 0), (left, CONV_W - 1 - left), (0, 0)))
    return b + sum(w[j] * up[:, j:j + T] for j in range(CONV_W))


def _lin_combine(e1, e2):
    a1, b1 = e1
    a2, b2 = e2
    return a1 * a2, a2 * b1 + b2


def linear_scan(a, b, h0, reverse):
    if reverse:
        a, b = a[:, ::-1], b[:, ::-1]
    b = b.at[:, 0].add(a[:, 0] * h0)
    _, h = lax.associative_scan(_lin_combine, (a, b), axis=1)
    return h[:, ::-1] if reverse else h


def lru_gates(u, w_rec, b_rec, w_inp, b_inp, lam):
    B, T, _ = u.shape
    ub = u.reshape(B, T, LRU_BLOCKS, LRU_BLOCK_DIM)
    r = jax.nn.sigmoid(jnp.einsum('bthi,hij->bthj', ub, w_rec).reshape(B, T, D_LRU) + b_rec)
    i = jax.nn.sigmoid(jnp.einsum('bthi,hij->bthj', ub, w_inp).reshape(B, T, D_LRU) + b_inp)
    log_a = -LRU_C * r * jax.nn.softplus(-lam)
    return jnp.exp(log_a), jnp.sqrt(-jnp.expm1(2.0 * log_a)) * (i * u)


def lru_direction(u_lat, u_ctx, w_rec, b_rec, w_inp, b_inp, lam, reverse):
    w_rec, b_rec, w_inp, b_inp, lam = (p.astype(F32) for p in (w_rec, b_rec, w_inp, b_inp, lam))
    a_c, b_c = lru_gates(u_ctx, w_rec, b_rec, w_inp, b_inp, lam)
    h_ctx = linear_scan(a_c, b_c, jnp.zeros_like(b_c[:, 0]), reverse)
    h_end = h_ctx[:, 0] if reverse else h_ctx[:, -1]
    a_x, b_x = lru_gates(u_lat, w_rec, b_rec, w_inp, b_inp, lam)
    return linear_scan(a_x, b_x, h_end, reverse), h_ctx


def rope_1d(x, pos):
    half = x.shape[-1] // 2
    freqs = ROPE_BASE ** (-jnp.arange(half, dtype=F32) / half)
    ang = pos[:, None] * freqs[None, :]
    cos = jnp.cos(ang)[None, :, None, :]
    sin = jnp.sin(ang)[None, :, None, :]
    x1, x2 = x[..., :half], x[..., half:]
    return jnp.concatenate([x1 * cos - x2 * sin, x1 * sin + x2 * cos], axis=-1)


def rope_2d(x, row, col):
    h = x.shape[-1] // 2
    return jnp.concatenate([rope_1d(x[..., :h], row), rope_1d(x[..., h:], col)], axis=-1)


def head_rms(o):
    return o * lax.rsqrt(jnp.mean(o * o, axis=-1, keepdims=True) + EPS)


def retention_chunkwise(q, k, v, log_g, s0, inclusive):
    B, T, H, dk = q.shape
    dv = v.shape[-1]
    C = RET_CHUNK
    n = T // C
    qc = q.reshape(B, n, C, H, dk)
    kc = k.reshape(B, n, C, H, dk)
    vc = v.reshape(B, n, C, H, dv)
    idx = jnp.arange(C, dtype=F32)
    diff = idx[:, None] - idx[None, :]
    mask = (diff >= 0) if inclusive else (diff > 0)
    dmat = jnp.where(mask[None], jnp.exp(jnp.where(mask, diff, 0.0)[None] * log_g[:, None, None]), 0.0)
    scores = jnp.einsum('bnihd,bnjhd->bnhij', qc, kc) * dmat
    intra = jnp.einsum('bnhij,bnjhe->bnihe', scores, vc)
    zeta = jnp.exp((C - 1 - idx)[:, None] * log_g[None, :])
    xi = jnp.exp((idx + 1.0)[:, None] * log_g[None, :])
    chunk_kv = jnp.einsum('bnjhd,jh,bnjhe->nbhde', kc, zeta, vc)
    decay_chunk = jnp.exp(C * log_g)[None, :, None, None]

    def step(state, kv):
        return decay_chunk * state + kv, state

    _, s_prev = lax.scan(step, s0, chunk_kv)
    cross = jnp.einsum('bnihd,ih,nbhde->bnihe', qc, xi, s_prev)
    return (intra + cross).reshape(B, T, H, dv)


def context_states(k, v, log_g):
    L = k.shape[1]
    pos = jnp.arange(L, dtype=F32)
    wf = jnp.exp((L - 1 - pos)[:, None] * log_g[None, :])
    wb = jnp.exp(pos[:, None] * log_g[None, :])
    s_f = jnp.einsum('bmhd,mh,bmhe->bhde', k, wf, v)
    s_b = jnp.einsum('bmhd,mh,bmhe->bhde', k, wb, v)
    return s_f, s_b


def retention_context(q, k, v, log_g):
    L = q.shape[1]
    pos = jnp.arange(L, dtype=F32)
    dist = jnp.abs(pos[:, None] - pos[None, :])
    dmat = jnp.exp(dist[None] * log_g[:, None, None])
    scores = jnp.einsum('bnhd,bmhd->bhnm', q, k) * dmat
    return jnp.einsum('bhnm,bmhe->bnhe', scores, v)


def mixer(hx, hc, w_in, conv_w, conv_b, w_rec, b_rec, w_inp, b_inp, lam, w_out, row, col, log_g, need_ctx_out):
    B, T, _ = hx.shape
    L = hc.shape[1]
    xa, ga, q, k, v, g = jnp.split((hx @ w_in).astype(F32), 6, axis=-1)
    xa_c, ga_c, q_c, k_c, v_c, g_c = jnp.split((hc @ w_in).astype(F32), 6, axis=-1)
    cw, cb = conv_w.astype(F32), conv_b.astype(F32)
    u, u_c = dwconv(xa, cw, cb), dwconv(xa_c, cw, cb)
    h_f, hc_f = lru_direction(u, u_c, w_rec[0], b_rec[0], w_inp[0], b_inp[0], lam[0], False)
    h_b, hc_b = lru_direction(u, u_c, w_rec[1], b_rec[1], w_inp[1], b_inp[1], lam[1], True)
    y_lru = (h_f + h_b) * jax.nn.gelu(ga)
    scale = RET_HEAD_DIM ** -0.5
    qh = rope_2d(q.reshape(B, T, RET_HEADS, RET_HEAD_DIM), row, col)
    kh = rope_2d(k.reshape(B, T, RET_HEADS, RET_HEAD_DIM), row, col) * scale
    vh = v.reshape(B, T, RET_HEADS, RET_HEAD_DIM)
    kh_c = k_c.reshape(B, L, RET_HEADS, RET_HEAD_DIM) * scale
    vh_c = v_c.reshape(B, L, RET_HEADS, RET_HEAD_DIM)
    s_f, s_b = context_states(kh_c, vh_c, log_g)
    o = retention_chunkwise(qh, kh, vh, log_g, s_f, True) + \
        retention_chunkwise(qh[:, ::-1], kh[:, ::-1], vh[:, ::-1], log_g, s_b, False)[:, ::-1]
    y_ret = head_rms(o).reshape(B, T, D_RET) * jax.nn.silu(g)
    out = (jnp.concatenate([y_lru, y_ret], axis=-1) @ w_out.astype(F32)).astype(hx.dtype)
    if not need_ctx_out:
        return out, None
    qh_c = q_c.reshape(B, L, RET_HEADS, RET_HEAD_DIM)
    y_lru_c = (hc_f + hc_b) * jax.nn.gelu(ga_c)
    y_ret_c = head_rms(retention_context(qh_c, kh_c, vh_c, log_g)).reshape(B, L, D_RET) * jax.nn.silu(g_c)
    out_c = (jnp.concatenate([y_lru_c, y_ret_c], axis=-1) @ w_out.astype(F32)).astype(hc.dtype)
    return out, out_c


def routed_experts(x, eidx, gates, w1, w3, w2):
    N, D = x.shape
    NK = N * TOP_K
    e_flat = eidx.reshape(-1)
    tok_flat = jnp.repeat(jnp.arange(N, dtype=jnp.int32), TOP_K)
    g_flat = gates.reshape(-1)
    order = jnp.argsort(e_flat)
    e_sorted, tok_sorted, g_sorted = e_flat[order], tok_flat[order], g_flat[order]
    counts = jnp.bincount(e_flat, length=N_EXPERTS)
    starts = jnp.cumsum(counts) - counts
    padded = (counts + MOE_BLOCK - 1) // MOE_BLOCK * MOE_BLOCK
    pends = jnp.cumsum(padded)
    pstarts = pends - padded
    dest = pstarts[e_sorted] + jnp.arange(NK, dtype=jnp.int32) - starts[e_sorted]
    n_blocks = -(-NK // MOE_BLOCK) + N_EXPERTS
    P = n_blocks * MOE_BLOCK
    tok_buf = jnp.zeros((P,), jnp.int32).at[dest].set(tok_sorted)
    g_buf = jnp.zeros((P,), F32).at[dest].set(g_sorted)
    block_expert = jnp.clip(jnp.searchsorted(pends, jnp.arange(n_blocks) * MOE_BLOCK, side='right'), 0, N_EXPERTS - 1)

    def run_block(args):
        tok, gw, e = args
        return swiglu(x[tok], w1[e], w3[e], w2[e]).astype(F32) * gw[:, None]

    y = lax.map(run_block, (tok_buf.reshape(n_blocks, MOE_BLOCK), g_buf.reshape(n_blocks, MOE_BLOCK), block_expert))
    return jax.ops.segment_sum(y.reshape(P, D), tok_buf, num_segments=N)


def moe(h, w_router, b_router, w1, w3, w2, ws1, ws3, ws2):
    B, T, D = h.shape
    x = h.reshape(B * T, D)
    N = x.shape[0]
    s = jax.nn.sigmoid(x.astype(F32) @ w_router.astype(F32))
    sb = s + b_router.astype(F32)
    grp = sb.reshape(N, N_GROUPS, N_EXPERTS // N_GROUPS)
    gscore = lax.top_k(grp, 2)[0].sum(-1)
    _, gidx = lax.top_k(gscore, TOP_GROUPS)
    gmask = jnp.repeat(jax.nn.one_hot(gidx, N_GROUPS, dtype=F32).sum(1), N_EXPERTS // N_GROUPS, axis=1)
    _, eidx = lax.top_k(jnp.where(gmask > 0, sb, -jnp.inf), TOP_K)
    sel = jnp.take_along_axis(s, eidx, axis=1)
    gates = sel / jnp.sum(sel, axis=-1, keepdims=True) * ROUTE_SCALE
    routed = routed_experts(x, eidx, gates, w1, w3, w2)
    shared = swiglu(x, ws1, ws3, ws2).astype(F32)
    return (routed + shared).astype(h.dtype).reshape(B, T, D)


def setup_inputs(seed: int = 0) -> dict:
    key = jax.random.key(seed)
    ks = jax.random.split(key, 28)

    def nrm(k, shape, scale):
        return jax.random.normal(k, shape, F32) * scale

    u = jax.random.uniform(ks[15], (DEPTH, 2, D_LRU), F32, minval=0.9, maxval=0.999)
    s = u ** (1.0 / LRU_C)
    return {
        'x': nrm(ks[0], (BATCH, SEQ, D_MODEL), 1.0),
        'c': nrm(ks[1], (BATCH, D_MODEL), 1.0),
        'ctx': nrm(ks[2], (BATCH, CTX_LEN, D_MODEL), 1.0),
        'c_ctx': nrm(ks[3], (D_MODEL,), 1.0),
        'w_mod': nrm(ks[4], (DEPTH, D_MODEL, N_MOD * D_MODEL), D_MODEL ** -0.5),
        'b_mod': nrm(ks[5], (DEPTH, N_MOD * D_MODEL), 0.01),
        'norm_mix': 1.0 + nrm(ks[6], (DEPTH, D_MODEL), 0.01),
        'norm_ffn': 1.0 + nrm(ks[7], (DEPTH, D_MODEL), 0.01),
        'w_in': nrm(ks[8], (DEPTH, D_MODEL, 2 * D_LRU + 4 * D_RET), D_MODEL ** -0.5),
        'conv_w': nrm(ks[9], (DEPTH, CONV_W, D_LRU), CONV_W ** -0.5),
        'conv_b': nrm(ks[10], (DEPTH, D_LRU), 0.01),
        'lru_w_rec': nrm(ks[11], (DEPTH, 2, LRU_BLOCKS, LRU_BLOCK_DIM, LRU_BLOCK_DIM), LRU_BLOCK_DIM ** -0.5),
        'lru_b_rec': nrm(ks[12], (DEPTH, 2, D_LRU), 0.01),
        'lru_w_in': nrm(ks[13], (DEPTH, 2, LRU_BLOCKS, LRU_BLOCK_DIM, LRU_BLOCK_DIM), LRU_BLOCK_DIM ** -0.5),
        'lru_b_in': nrm(ks[14], (DEPTH, 2, D_LRU), 0.01),
        'lru_lambda': jnp.log(s) - jnp.log1p(-s),
        'w_out': nrm(ks[16], (DEPTH, D_MIX, D_MODEL), D_MIX ** -0.5),
        'w_router': nrm(ks[17], (DEPTH, D_MODEL, N_EXPERTS), D_MODEL ** -0.5),
        'b_router': nrm(ks[18], (DEPTH, N_EXPERTS), 0.01),
        'w_exp_gate': nrm(ks[19], (DEPTH, N_EXPERTS, D_MODEL, D_EXPERT), D_MODEL ** -0.5),
        'w_exp_up': nrm(ks[20], (DEPTH, N_EXPERTS, D_MODEL, D_EXPERT), D_MODEL ** -0.5),
        'w_exp_down': nrm(ks[21], (DEPTH, N_EXPERTS, D_EXPERT, D_MODEL), D_EXPERT ** -0.5),
        'w_sh_gate': nrm(ks[22], (DEPTH, D_MODEL, D_SHARED), D_MODEL ** -0.5),
        'w_sh_up': nrm(ks[23], (DEPTH, D_MODEL, D_SHARED), D_MODEL ** -0.5),
        'w_sh_down': nrm(ks[24], (DEPTH, D_SHARED, D_MODEL), D_SHARED ** -0.5),
        'norm_final': 1.0 + nrm(ks[25], (D_MODEL,), 0.01),
    }


def reference(x, c, ctx, c_ctx, w_mod, b_mod, norm_mix, norm_ffn, w_in, conv_w, conv_b, lru_w_rec, lru_b_rec, lru_w_in, lru_b_in, lru_lambda, w_out, w_router, b_router, w_exp_gate, w_exp_up, w_exp_down, w_sh_gate, w_sh_up, w_sh_down, norm_final):
    T = x.shape[1]
    rows = T // GRID_W
    row = jnp.repeat(jnp.arange(rows, dtype=F32), GRID_W)
    col = jnp.tile(jnp.arange(GRID_W, dtype=F32), rows)
    log_g = jnp.log1p(-jnp.exp2(-(5.0 + 2.0 * jnp.arange(RET_HEADS, dtype=F32))))
    for l in range(DEPTH):
        last = l == DEPTH - 1
        mod = jax.nn.silu(c) @ w_mod[l] + b_mod[l]
        sh1, sc1, g1, sh2, sc2, g2 = jnp.split(mod[:, None, :], N_MOD, axis=-1)
        mod_c = jax.nn.silu(c_ctx) @ w_mod[l] + b_mod[l]
        csh1, csc1, cg1, csh2, csc2, cg2 = jnp.split(mod_c, N_MOD, axis=-1)
        hx = modulate(rmsnorm(x, norm_mix[l]), sh1, sc1)
        hc = modulate(rmsnorm(ctx, norm_mix[l]), csh1, csc1)
        mx, mc = mixer(hx, hc, w_in[l], conv_w[l], conv_b[l], lru_w_rec[l], lru_b_rec[l], lru_w_in[l], lru_b_in[l], lru_lambda[l], w_out[l], row, col, log_g, not last)
        x = x + g1 * mx
        ffn_args = (w_router[l], b_router[l], w_exp_gate[l], w_exp_up[l], w_exp_down[l], w_sh_gate[l], w_sh_up[l], w_sh_down[l])
        x = x + g2 * moe(modulate(rmsnorm(x, norm_ffn[l]), sh2, sc2), *ffn_args)
        if not last:
            ctx = ctx + cg1 * mc
            ctx = ctx + cg2 * moe(modulate(rmsnorm(ctx, norm_ffn[l]), csh2, csc2), *ffn_args)
    return rmsnorm(x, norm_final)
```

```python
import functools

import jax
import jax.numpy as jnp
from jax import lax
from jax.experimental import pallas as pl
from jax.experimental.pallas import tpu as pltpu

F32 = jnp.float32
BF16 = jnp.bfloat16
HIGHEST = lax.Precision.HIGHEST

EPS = 1e-6
GRID_W = 64
LRU_C = 8.0
RET_HEADS = 4
RET_CHUNK = 128
ROPE_BASE = 10000.0
N_GROUPS = 8
TOP_GROUPS = 4
TOP_K = 6
ROUTE_SCALE = 2.5

LANES = 128
SUBLANES = 8
VMEM_LIMIT_BYTES = 56 * 1024 * 1024

NT_DIMS = (((1,), (1,)), ((), ()))


def _cparams(*sem):
    return pltpu.CompilerParams(dimension_semantics=sem, vmem_limit_bytes=VMEM_LIMIT_BYTES)


def _silu(v):
    return v * jax.nn.sigmoid(v)


def _rms_mod(x, gain, shift, scale):
    y = x * lax.rsqrt(jnp.mean(x * x, axis=-1, keepdims=True) + EPS) * gain
    return y * (1.0 + scale) + shift


def _mod_kernel(c_ref, w_ref, b_ref, o_ref):
    s = _silu(c_ref[...])
    o_ref[...] = jnp.dot(s, w_ref[...], precision=HIGHEST, preferred_element_type=F32) + b_ref[...]


def _mod_call(cc, w_mod, b_mod):
    rows, d = cc.shape
    n = w_mod.shape[1]
    tn = 512
    return pl.pallas_call(
        _mod_kernel,
        grid=(n // tn,),
        in_specs=[
            pl.BlockSpec((rows, d), lambda j: (0, 0)),
            pl.BlockSpec((d, tn), lambda j: (0, j)),
            pl.BlockSpec((1, tn), lambda j: (0, j)),
        ],
        out_specs=pl.BlockSpec((rows, tn), lambda j: (0, j)),
        out_shape=jax.ShapeDtypeStruct((rows, n), F32),
        compiler_params=_cparams("arbitrary"),
        name="mod",
    )(cc, w_mod, b_mod.reshape(1, n))


def _inproj_kernel(x_ref, sh_ref, sc_ref, g_ref, w_ref, xg_ref, qkvg_ref, *, n_xg):
    h = _rms_mod(x_ref[0], g_ref[...], sh_ref[0], sc_ref[0]).astype(BF16)
    xg_ref[0] = jnp.dot(h, w_ref[:, :n_xg], preferred_element_type=F32)
    qkvg_ref[0] = jnp.dot(h, w_ref[:, n_xg:], preferred_element_type=F32)


def _inproj_call(x, shift, scale, gain, w_in_bf, n_xg, per_batch_mod):
    b, t, d = x.shape
    n_all = w_in_bf.shape[1]
    tt = min(t, 512)
    mod_map = (lambda bi, i: (bi, 0, 0)) if per_batch_mod else (lambda bi, i: (0, 0, 0))
    return pl.pallas_call(
        functools.partial(_inproj_kernel, n_xg=n_xg),
        grid=(b, t // tt),
        in_specs=[
            pl.BlockSpec((1, tt, d), lambda bi, i: (bi, i, 0)),
            pl.BlockSpec((1, 1, d), mod_map),
            pl.BlockSpec((1, 1, d), mod_map),
            pl.BlockSpec((1, d), lambda bi, i: (0, 0)),
            pl.BlockSpec((d, n_all), lambda bi, i: (0, 0)),
        ],
        out_specs=[
            pl.BlockSpec((1, tt, n_xg), lambda bi, i: (bi, i, 0)),
            pl.BlockSpec((1, tt, n_all - n_xg), lambda bi, i: (bi, i, 0)),
        ],
        out_shape=[
            jax.ShapeDtypeStruct((b, t, n_xg), F32),
            jax.ShapeDtypeStruct((b, t, n_all - n_xg), F32),
        ],
        compiler_params=_cparams("parallel", "arbitrary"),
        name="inproj",
    )(x, shift, scale, gain, w_in_bf)


SCAN_ROWS = 64


def _tile_scan(a, b, reverse):
    row = lax.broadcasted_iota(jnp.int32, a.shape, 0)
    for s in (1, 2, 4):
        if reverse:
            a_s = pltpu.roll(a, SUBLANES - s, 0)
            b_s = pltpu.roll(b, SUBLANES - s, 0)
            m = row < SUBLANES - s
        else:
            a_s = pltpu.roll(a, s, 0)
            b_s = pltpu.roll(b, s, 0)
            m = row >= s
        b = jnp.where(m, a * b_s + b, b)
        a = jnp.where(m, a * a_s, a)
    return a, b


def _lru_kernel(xa_ref, ga_ref, xac_ref, cw_ref, cb_ref, wg_ref, bg_ref, lam_ref, y_ref,
                hf_ref, xe_ref, a_ref, b_ref, carry_ref, *, t_len, c_len, tc, nc):
    j = pl.program_id(1)
    half = wg_ref.shape[2]

    def gates(n, dirn):
        u = cb_ref[...]
        for k in range(cw_ref.shape[0]):
            u = u + cw_ref[k:k + 1, :] * xe_ref[6 + k:6 + k + n, :]
        lam = -lam_ref[dirn:dirn + 1, :]
        sp = jnp.maximum(lam, 0.0) + jnp.log1p(jnp.exp(-jnp.abs(lam)))
        for hf in range(2):
            uh = u[:, hf * half:(hf + 1) * half]
            z = jnp.dot(uh.astype(BF16), wg_ref[dirn, hf], preferred_element_type=F32) + bg_ref[dirn, hf]
            r = jax.nn.sigmoid(z[:, :half])
            i = jax.nn.sigmoid(z[:, half:])
            log_a = -LRU_C * r * sp[:, hf * half:(hf + 1) * half]
            a = jnp.exp(log_a)
            a_ref[0:n, hf * half:(hf + 1) * half] = a
            b_ref[0:n, hf * half:(hf + 1) * half] = jnp.sqrt((1.0 - a) * (1.0 + a)) * (i * uh)

    def scan(n, dirn, emit):
        ng = n // SCAN_ROWS
        nt = SCAN_ROWS // SUBLANES

        def body(gi, carry):
            g = gi if dirn == 0 else ng - 1 - gi
            base = pl.multiple_of(g * SCAN_ROWS, SCAN_ROWS)
            av = a_ref[pl.ds(base, SCAN_ROWS), :]
            bv = b_ref[pl.ds(base, SCAN_ROWS), :]
            hs = [None] * nt
            order = range(nt) if dirn == 0 else range(nt - 1, -1, -1)
            for k in order:
                a, b = _tile_scan(av[k * SUBLANES:(k + 1) * SUBLANES], bv[k * SUBLANES:(k + 1) * SUBLANES],
                                  dirn == 1)
                h = a * carry + b
                carry = h[SUBLANES - 1:SUBLANES] if dirn == 0 else h[0:1]
                hs[k] = h
            if emit is not None:
                emit(base, jnp.concatenate(hs, axis=0))
            return carry

        carry_ref[...] = lax.fori_loop(0, ng, body, carry_ref[...])

    def run(dirn):
        first = (j == 0) if dirn == 0 else (j == nc)
        ci = j if dirn == 0 else 2 * nc - 1 - j

        @pl.when(first)
        def _():
            zeros = jnp.zeros((SUBLANES, xe_ref.shape[1]), F32)
            xe_ref[0:SUBLANES] = zeros
            xe_ref[SUBLANES:SUBLANES + c_len] = xac_ref[0]
            xe_ref[SUBLANES + c_len:2 * SUBLANES + c_len] = zeros
            gates(c_len, dirn)
            carry_ref[...] = jnp.zeros(carry_ref.shape, F32)
            scan(c_len, dirn, None)

        r0 = pl.multiple_of(ci * tc, tc)
        p0 = pl.multiple_of(jnp.maximum(r0 - SUBLANES, 0), SUBLANES)
        n0 = pl.multiple_of(jnp.minimum(r0 + tc, t_len - SUBLANES), SUBLANES)
        prev = xa_ref[0, pl.ds(p0, SUBLANES), :]
        nxt = xa_ref[0, pl.ds(n0, SUBLANES), :]
        xe_ref[0:SUBLANES] = jnp.where(ci > 0, prev, 0.0)
        xe_ref[SUBLANES:SUBLANES + tc] = xa_ref[0, pl.ds(r0, tc), :]
        xe_ref[SUBLANES + tc:2 * SUBLANES + tc] = jnp.where(ci < nc - 1, nxt, 0.0)
        gates(tc, dirn)

        if dirn == 0:
            def emit(base, h):
                hf_ref[pl.ds(r0 + base, SCAN_ROWS), :] = h
        else:
            def emit(base, h):
                tot = hf_ref[pl.ds(r0 + base, SCAN_ROWS), :] + h
                ga = ga_ref[0, pl.ds(base, SCAN_ROWS), :]
                y_ref[0, pl.ds(base, SCAN_ROWS), :] = (tot * jax.nn.gelu(ga)).astype(y_ref.dtype)
        scan(tc, dirn, emit)

    @pl.when(j < nc)
    def _():
        run(0)

    @pl.when(j >= nc)
    def _():
        run(1)


def _lru_call(xg, xg_c, conv_w, conv_b, wg, bg, lam):
    b, t, _ = xg.shape
    c_len = xg_c.shape[1]
    d_lru = conv_w.shape[1]
    tc = min(t, 256)
    nc = t // tc
    assert c_len <= tc and c_len % SCAN_ROWS == 0 and tc % SCAN_ROWS == 0

    def chunk_map(bi, j):
        return (bi, jnp.where(j < nc, nc - 1, 2 * nc - 1 - j), 0)

    def ga_map(bi, j):
        return (bi, jnp.where(j < nc, nc - 1, 2 * nc - 1 - j), 1)

    return pl.pallas_call(
        functools.partial(_lru_kernel, t_len=t, c_len=c_len, tc=tc, nc=nc),
        grid=(b, 2 * nc),
        in_specs=[
            pl.BlockSpec((1, t, d_lru), lambda bi, j: (bi, 0, 0)),
            pl.BlockSpec((1, tc, d_lru), ga_map),
            pl.BlockSpec((1, c_len, d_lru), lambda bi, j: (bi, 0, 0)),
            pl.BlockSpec(conv_w.shape, lambda bi, j: (0, 0)),
            pl.BlockSpec((1, d_lru), lambda bi, j: (0, 0)),
            pl.BlockSpec(wg.shape, lambda bi, j: (0, 0, 0, 0)),
            pl.BlockSpec(bg.shape, lambda bi, j: (0, 0, 0, 0)),
            pl.BlockSpec(lam.shape, lambda bi, j: (0, 0)),
        ],
        out_specs=pl.BlockSpec((1, tc, d_lru), chunk_map),
        out_shape=jax.ShapeDtypeStruct((b, t, d_lru), BF16),
        scratch_shapes=[
            pltpu.VMEM((t, d_lru), F32),
            pltpu.VMEM((tc + 2 * SUBLANES, d_lru), F32),
            pltpu.VMEM((tc, d_lru), F32),
            pltpu.VMEM((tc, d_lru), F32),
            pltpu.VMEM((1, d_lru), F32),
        ],
        compiler_params=_cparams("parallel", "arbitrary"),
        name="lru",
    )(xg, xg, xg_c, conv_w, conv_b.reshape(1, d_lru), wg, bg, lam)


def _ret_kernel(q_ref, k_ref, v_ref, g_ref, kc_ref, vc_ref, cos_ref, sin_ref, dm_ref, rt_ref, cwt_ref, gc_ref,
                y_ref, qr_ref, kr_ref, sf_ref, sb_ref, *, nc, scale):
    C = RET_CHUNK
    dh = q_ref.shape[2]
    lane = lax.broadcasted_iota(jnp.int32, (C, dh), 1)
    low = (lane % (dh // 2)) < (dh // 4)

    def rope(xv, cs, sn):
        partner = jnp.where(low, pltpu.roll(xv, dh - dh // 4, 1), pltpu.roll(xv, dh // 4, 1))
        return xv * cs + partner * sn

    xi_f, xi_b, ze_f, ze_b = rt_ref[0, 0], rt_ref[0, 1], rt_ref[0, 2], rt_ref[0, 3]
    g_chunk = gc_ref[0]
    g_full = jnp.concatenate([g_chunk] * (dh // SUBLANES), axis=0)

    kc = (kc_ref[0] * scale).T.astype(BF16)
    vc = vc_ref[0]
    s_f0 = jnp.dot(kc, (vc * cwt_ref[0, 0]).astype(BF16), preferred_element_type=F32)
    s_b0 = jnp.dot(kc, (vc * cwt_ref[0, 1]).astype(BF16), preferred_element_type=F32)

    def p1(n, s):
        r = pl.multiple_of(n * C, C)
        cs, sn = cos_ref[pl.ds(r, C), :], sin_ref[pl.ds(r, C), :]
        k = rope(k_ref[0, pl.ds(r, C), :], cs, sn) * scale
        qr_ref[pl.ds(r, C), :] = rope(q_ref[0, pl.ds(r, C), :], cs, sn)
        kr_ref[pl.ds(r, C), :] = k.astype(BF16)
        v = v_ref[0, pl.ds(r, C), :]
        vz = jnp.concatenate([v * ze_f, v * ze_b], axis=1).astype(BF16)
        kv = jnp.dot(k.T.astype(BF16), vz, preferred_element_type=F32)
        sf_ref[n] = s
        sb_ref[n] = kv[:, dh:]
        return g_full * s + kv[:, :dh]

    lax.fori_loop(0, nc, p1, s_f0)

    def p1b(i, s):
        n = nc - 1 - i
        kvb = sb_ref[n]
        sb_ref[n] = s
        return g_full * s + kvb

    lax.fori_loop(0, nc, p1b, s_b0)

    def p2(n, carry):
        r = pl.multiple_of(n * C, C)
        q = qr_ref[pl.ds(r, C), :]
        kb = kr_ref[pl.ds(r, C), :]
        vb = v_ref[0, pl.ds(r, C), :].astype(BF16)
        sc = lax.dot_general(q.astype(BF16), kb, NT_DIMS, preferred_element_type=F32) * dm_ref[0]
        o = jnp.dot(sc.astype(BF16), vb, preferred_element_type=F32)
        qx = jnp.concatenate([q * xi_f, q * xi_b], axis=1).astype(BF16)
        st = jnp.concatenate([sf_ref[n], sb_ref[n]], axis=0).astype(BF16)
        o = o + jnp.dot(qx, st, preferred_element_type=F32)
        o = o * lax.rsqrt(jnp.mean(o * o, axis=-1, keepdims=True) + EPS)
        y_ref[0, pl.ds(r, C), :] = (o * _silu(g_ref[0, pl.ds(r, C), :])).astype(y_ref.dtype)
        return carry

    lax.fori_loop(0, nc, p2, 0)


def _ret_tables(t, c_len, dh):
    C = RET_CHUNK
    hh = jnp.arange(RET_HEADS, dtype=F32)
    log_g = jnp.log1p(-jnp.exp2(-(5.0 + 2.0 * hh)))
    idx = jnp.arange(C, dtype=F32)
    dm = jnp.exp(jnp.abs(idx[:, None] - idx[None, :])[None] * log_g[:, None, None])

    def lanes(e):
        return jnp.broadcast_to(jnp.exp(e[None, :] * log_g[:, None])[:, :, None], (RET_HEADS, e.shape[0], dh))

    rt = jnp.stack([lanes(idx + 1.0), lanes(C - idx), lanes(C - 1.0 - idx), lanes(idx)], axis=1)
    pos = jnp.arange(c_len, dtype=F32)
    cwt = jnp.stack([lanes(c_len - 1.0 - pos), lanes(pos)], axis=1)
    gc = jnp.broadcast_to(jnp.exp(C * log_g)[:, None, None], (RET_HEADS, SUBLANES, dh))
    rows = t // GRID_W
    row = jnp.repeat(jnp.arange(rows, dtype=F32), GRID_W)
    col = jnp.tile(jnp.arange(GRID_W, dtype=F32), rows)
    q4 = dh // 4
    freqs = ROPE_BASE ** (-jnp.arange(q4, dtype=F32) / q4)
    ang_r = row[:, None] * freqs[None, :]
    ang_c = col[:, None] * freqs[None, :]
    cos = jnp.concatenate([jnp.cos(ang_r)] * 2 + [jnp.cos(ang_c)] * 2, axis=1)
    sin = jnp.concatenate([-jnp.sin(ang_r), jnp.sin(ang_r), -jnp.sin(ang_c), jnp.sin(ang_c)], axis=1)
    return dm, rt, cwt, gc, cos, sin


def _ret_call(qkvg, qkvg_c):
    b, t, w = qkvg.shape
    c_len = qkvg_c.shape[1]
    dh = w // (4 * RET_HEADS)
    nc = t // RET_CHUNK
    dm, rt, cwt, gc, cos, sin = _ret_tables(t, c_len, dh)
    H = RET_HEADS

    def col(off):
        return lambda bi, h: (bi, 0, off + h)

    return pl.pallas_call(
        functools.partial(_ret_kernel, nc=nc, scale=dh ** -0.5),
        grid=(b, H),
        in_specs=[
            pl.BlockSpec((1, t, dh), col(0)),
            pl.BlockSpec((1, t, dh), col(H)),
            pl.BlockSpec((1, t, dh), col(2 * H)),
            pl.BlockSpec((1, t, dh), col(3 * H)),
            pl.BlockSpec((1, c_len, dh), col(H)),
            pl.BlockSpec((1, c_len, dh), col(2 * H)),
            pl.BlockSpec((t, dh), lambda bi, h: (0, 0)),
            pl.BlockSpec((t, dh), lambda bi, h: (0, 0)),
            pl.BlockSpec((1, RET_CHUNK, RET_CHUNK), lambda bi, h: (h, 0, 0)),
            pl.BlockSpec((1, 4, RET_CHUNK, dh), lambda bi, h: (h, 0, 0, 0)),
            pl.BlockSpec((1, 2, c_len, dh), lambda bi, h: (h, 0, 0, 0)),
            pl.BlockSpec((1, SUBLANES, dh), lambda bi, h: (h, 0, 0)),
        ],
        out_specs=pl.BlockSpec((1, t, dh), col(0)),
        out_shape=jax.ShapeDtypeStruct((b, t, H * dh), BF16),
        scratch_shapes=[
            pltpu.VMEM((t, dh), F32),
            pltpu.VMEM((t, dh), BF16),
            pltpu.VMEM((nc, dh, dh), F32),
            pltpu.VMEM((nc, dh, dh), F32),
        ],
        compiler_params=_cparams("parallel", "arbitrary"),
        name="retention",
    )(qkvg, qkvg, qkvg, qkvg, qkvg_c, qkvg_c, cos, sin, dm, rt, cwt, gc)


def _route(logits_t, bias):
    e, n = logits_t.shape
    per = e // N_GROUPS
    s = jax.nn.sigmoid(logits_t)
    sb = s + bias
    neg = -jnp.inf
    g3 = sb.reshape(N_GROUPS, per, n)
    it3 = lax.broadcasted_iota(jnp.int32, g3.shape, 1)
    m1 = jnp.max(g3, axis=1, keepdims=True)
    i1 = jnp.min(jnp.where(g3 == m1, it3, per), axis=1, keepdims=True)
    m2 = jnp.max(jnp.where(it3 == i1, neg, g3), axis=1, keepdims=True)
    gs = (m1 + m2).reshape(N_GROUPS, n)
    itg = lax.broadcasted_iota(jnp.int32, gs.shape, 0)
    gsel = jnp.zeros(gs.shape, jnp.bool_)
    for _ in range(TOP_GROUPS):
        m = jnp.max(gs, axis=0, keepdims=True)
        i = jnp.min(jnp.where(gs == m, itg, N_GROUPS), axis=0, keepdims=True)
        hit = itg == i
        gsel = gsel | hit
        gs = jnp.where(hit, neg, gs)
    gmask = jnp.broadcast_to(gsel.reshape(N_GROUPS, 1, n), (N_GROUPS, per, n)).reshape(e, n)
    v = jnp.where(gmask, sb, neg)
    ite = lax.broadcasted_iota(jnp.int32, v.shape, 0)
    sel = jnp.zeros(v.shape, jnp.bool_)
    for _ in range(TOP_K):
        m = jnp.max(v, axis=0, keepdims=True)
        i = jnp.min(jnp.where(v == m, ite, e), axis=0, keepdims=True)
        hit = ite == i
        sel = sel | hit
        v = jnp.where(hit, neg, v)
    picked = jnp.where(sel, s, 0.0)
    return picked / jnp.sum(picked, axis=0, keepdims=True) * ROUTE_SCALE


def _outproj_kernel(yl_ref, yr_ref, x_ref, g1_ref, sh_ref, sc_ref, nf_ref, wo_ref, wrt_ref, rb_ref,
                    x1_ref, h2_ref, gt_ref):
    dl = yl_ref.shape[2]
    mx = jnp.dot(yl_ref[0], wo_ref[0:dl, :], preferred_element_type=F32)
    mx = mx + jnp.dot(yr_ref[0], wo_ref[dl:, :], preferred_element_type=F32)
    x1 = x_ref[0] + g1_ref[0] * mx
    x1_ref[0] = x1
    h = _rms_mod(x1, nf_ref[...], sh_ref[0], sc_ref[0])
    h2_ref[0] = h.astype(h2_ref.dtype)
    logits_t = lax.dot_general(wrt_ref[...], h, NT_DIMS, precision=HIGHEST, preferred_element_type=F32)
    gates = _route(logits_t, rb_ref[...])
    gt_ref[...] = jnp.concatenate([gates, jnp.zeros((gt_ref.shape[0] - gates.shape[0], gates.shape[1]), F32)], axis=0)


def _outproj_call(y_lru, y_ret, x, g1, sh2, sc2, norm_ffn, w_out_bf, w_router_t, b_router):
    b, t, d = x.shape
    dl = y_lru.shape[2]
    e = w_router_t.shape[0]
    tt = min(t, 256)
    nt = t // tt
    mod_map = lambda bi, i: (bi, 0, 0)
    return pl.pallas_call(
        _outproj_kernel,
        grid=(b, nt),
        in_specs=[
            pl.BlockSpec((1, tt, dl), lambda bi, i: (bi, i, 0)),
            pl.BlockSpec((1, tt, d - dl), lambda bi, i: (bi, i, 0)),
            pl.BlockSpec((1, tt, d), lambda bi, i: (bi, i, 0)),
            pl.BlockSpec((1, 1, d), mod_map),
            pl.BlockSpec((1, 1, d), mod_map),
            pl.BlockSpec((1, 1, d), mod_map),
            pl.BlockSpec((1, d), lambda bi, i: (0, 0)),
            pl.BlockSpec((d, d), lambda bi, i: (0, 0)),
            pl.BlockSpec((e, d), lambda bi, i: (0, 0)),
            pl.BlockSpec((e, 1), lambda bi, i: (0, 0)),
        ],
        out_specs=[
            pl.BlockSpec((1, tt, d), lambda bi, i: (bi, i, 0)),
            pl.BlockSpec((1, tt, d), lambda bi, i: (bi, i, 0)),
            pl.BlockSpec((LANES, tt), lambda bi, i: (0, bi * nt + i)),
        ],
        out_shape=[
            jax.ShapeDtypeStruct((b, t, d), F32),
            jax.ShapeDtypeStruct((b, t, d), BF16),
            jax.ShapeDtypeStruct((LANES, b * t), F32),
        ],
        compiler_params=_cparams("parallel", "arbitrary"),
        name="outproj_route",
    )(y_lru, y_ret, x, g1, sh2, sc2, norm_ffn.reshape(1, d), w_out_bf, w_router_t, b_router.reshape(e, 1))


def _moe_kernel(h_ref, gt_ref, x1_ref, g2_ref, w1_ref, w3_ref, w2_ref, s1_ref, s3_ref, s2_ref, nfin_ref,
                o_ref, acc_ref, gl_ref):
    e = pl.program_id(1)
    h = h_ref[...]

    @pl.when(e == 0)
    def _():
        hs = _silu(jnp.dot(h, s1_ref[...], preferred_element_type=F32)) * jnp.dot(h, s3_ref[...], preferred_element_type=F32)
        acc_ref[...] = jnp.dot(hs.astype(BF16), s2_ref[...], preferred_element_type=F32)
        gl_ref[...] = gt_ref[...].T

    lane = lax.broadcasted_iota(jnp.int32, gl_ref.shape, 1)
    gate = jnp.sum(jnp.where(lane == e, gl_ref[...], 0.0), axis=1, keepdims=True)
    hh = _silu(jnp.dot(h, w1_ref[0], preferred_element_type=F32)) * jnp.dot(h, w3_ref[0], preferred_element_type=F32)
    acc_ref[...] += jnp.dot((hh * gate).astype(BF16), w2_ref[0], preferred_element_type=F32)

    @pl.when(e == pl.num_programs(1) - 1)
    def _():
        xo = x1_ref[...] + g2_ref[0] * acc_ref[...]
        o_ref[...] = xo * lax.rsqrt(jnp.mean(xo * xo, axis=-1, keepdims=True) + EPS) * nfin_ref[...]


def _moe_call(h2, gt, x1, g2, w1, w3, w2, ws1, ws3, ws2, norm_final, t):
    n, d = h2.shape
    ne, _, de = w1.shape
    tt = min(t, 1024)
    per_b = t // tt
    return pl.pallas_call(
        _moe_kernel,
        grid=(n // tt, ne),
        in_specs=[
            pl.BlockSpec((tt, d), lambda i, e: (i, 0)),
            pl.BlockSpec((LANES, tt), lambda i, e: (0, i)),
            pl.BlockSpec((tt, d), lambda i, e: (i, 0)),
            pl.BlockSpec((1, 1, d), lambda i, e: (i // per_b, 0, 0)),
            pl.BlockSpec((1, d, de), lambda i, e: (e, 0, 0)),
            pl.BlockSpec((1, d, de), lambda i, e: (e, 0, 0)),
            pl.BlockSpec((1, de, d), lambda i, e: (e, 0, 0)),
            pl.BlockSpec(ws1.shape, lambda i, e: (0, 0)),
            pl.BlockSpec(ws3.shape, lambda i, e: (0, 0)),
            pl.BlockSpec(ws2.shape, lambda i, e: (0, 0)),
            pl.BlockSpec((1, d), lambda i, e: (0, 0)),
        ],
        out_specs=pl.BlockSpec((tt, d), lambda i, e: (i, 0)),
        out_shape=jax.ShapeDtypeStruct((n, d), F32),
        scratch_shapes=[pltpu.VMEM((tt, d), F32), pltpu.VMEM((tt, LANES), F32)],
        compiler_params=_cparams("parallel", "arbitrary"),
        name="moe_dense",
    )(h2, gt, x1, g2, w1, w3, w2, ws1, ws3, ws2, norm_final.reshape(1, d))


def _block_diag(w):
    nb, n, _ = w.shape
    eye = jnp.eye(nb, dtype=w.dtype)
    return (eye[:, None, :, None] * w[:, :, None, :]).reshape(nb * n, nb * n)


def _lru_gate_weights(w_rec, b_rec, w_inp, b_inp):
    d_lru = b_rec.shape[1]
    half = d_lru // 2
    ws, bs = [], []
    for dirn in range(2):
        wr, wi = _block_diag(w_rec[dirn]), _block_diag(w_inp[dirn])
        ws.append(jnp.stack([jnp.concatenate([wr[s:s + half, s:s + half], wi[s:s + half, s:s + half]], axis=1)
                             for s in (0, half)]))
        bs.append(jnp.stack([jnp.concatenate([b_rec[dirn, s:s + half], b_inp[dirn, s:s + half]])[None, :]
                             for s in (0, half)]))
    return jnp.stack(ws).astype(BF16), jnp.stack(bs).astype(F32)


def kernel(x, c, ctx, c_ctx, w_mod, b_mod, norm_mix, norm_ffn, w_in, conv_w, conv_b, lru_w_rec, lru_b_rec, lru_w_in, lru_b_in, lru_lambda, w_out, w_router, b_router, w_exp_gate, w_exp_up, w_exp_down, w_sh_gate, w_sh_up, w_sh_down, norm_final):
    b, t, d = x.shape
    depth = w_mod.shape[0]
    assert depth == 1, "context-stream update for deeper stacks is not implemented"
    l = 0
    d_lru = conv_w.shape[2]
    n_xg = 2 * d_lru

    rows = -(-(b + 1) // SUBLANES) * SUBLANES
    cc = jnp.concatenate([c, c_ctx[None, :], jnp.zeros((rows - b - 1, d), F32)], axis=0)
    mod = _mod_call(cc, w_mod[l], b_mod[l]).reshape(rows, 1, -1)
    sh1, sc1, g1, sh2, sc2, g2 = (mod[:, :, i * d:(i + 1) * d] for i in range(6))

    w_in_bf = w_in[l].astype(BF16)
    gain_mix = norm_mix[l].reshape(1, d)
    xg, qkvg = _inproj_call(x, sh1[:b], sc1[:b], gain_mix, w_in_bf, n_xg, True)
    xg_c, qkvg_c = _inproj_call(ctx, sh1[b:b + 1], sc1[b:b + 1], gain_mix, w_in_bf, n_xg, False)

    wg, bg = _lru_gate_weights(lru_w_rec[l], lru_b_rec[l], lru_w_in[l], lru_b_in[l])
    y_lru = _lru_call(xg, xg_c, conv_w[l], conv_b[l], wg, bg, lru_lambda[l])
    y_ret = _ret_call(qkvg, qkvg_c)

    x1, h2, gt = _outproj_call(y_lru, y_ret, x, g1[:b], sh2[:b], sc2[:b], norm_ffn[l], w_out[l].astype(BF16),
                               w_router[l].T, b_router[l])

    out = _moe_call(h2.reshape(b * t, d), gt, x1.reshape(b * t, d), g2[:b],
                    w_exp_gate[l].astype(BF16), w_exp_up[l].astype(BF16), w_exp_down[l].astype(BF16),
                    w_sh_gate[l].astype(BF16), w_sh_up[l].astype(BF16), w_sh_down[l].astype(BF16), norm_final, t)
    return out.reshape(b, t, d)
```

```python
import functools

import jax
import jax.numpy as jnp
from jax import lax
from jax.experimental import pallas as pl
from jax.experimental.pallas import tpu as pltpu

F32 = jnp.float32
BF16 = jnp.bfloat16
HIGHEST = lax.Precision.HIGHEST

EPS = 1e-6
GRID_W = 64
LRU_C = 8.0
RET_HEADS = 4
RET_CHUNK = 128
ROPE_BASE = 10000.0
N_GROUPS = 8
TOP_GROUPS = 4
TOP_K = 6
ROUTE_SCALE = 2.5

LANES = 128
SUBLANES = 8
VMEM_LIMIT_BYTES = 56 * 1024 * 1024

NT_DIMS = (((1,), (1,)), ((), ()))


def _cparams(*sem):
    return pltpu.CompilerParams(dimension_semantics=sem, vmem_limit_bytes=VMEM_LIMIT_BYTES)


def _silu(v):
    return v * jax.nn.sigmoid(v)


def _rms_mod(x, gain, shift, scale):
    y = x * lax.rsqrt(jnp.mean(x * x, axis=-1, keepdims=True) + EPS) * gain
    return y * (1.0 + scale) + shift


def _mod_kernel(c_ref, w_ref, b_ref, o_ref):
    s = _silu(c_ref[...])
    o_ref[...] = jnp.dot(s, w_ref[...], precision=HIGHEST, preferred_element_type=F32) + b_ref[...]


def _mod_call(cc, w_mod, b_mod):
    rows, d = cc.shape
    n = w_mod.shape[1]
    tn = 512
    return pl.pallas_call(
        _mod_kernel,
        grid=(n // tn,),
        in_specs=[
            pl.BlockSpec((rows, d), lambda j: (0, 0)),
            pl.BlockSpec((d, tn), lambda j: (0, j)),
            pl.BlockSpec((1, tn), lambda j: (0, j)),
        ],
        out_specs=pl.BlockSpec((rows, tn), lambda j: (0, j)),
        out_shape=jax.ShapeDtypeStruct((rows, n), F32),
        compiler_params=_cparams("arbitrary"),
        name="mod",
    )(cc, w_mod, b_mod.reshape(1, n))


def _inproj_kernel(x_ref, sh_ref, sc_ref, g_ref, w_ref, xg_ref, qkvg_ref, *, n_xg):
    h = _rms_mod(x_ref[0], g_ref[...], sh_ref[0], sc_ref[0]).astype(BF16)
    xg_ref[0] = jnp.dot(h, w_ref[:, :n_xg], preferred_element_type=F32)
    qkvg_ref[0] = jnp.dot(h, w_ref[:, n_xg:], preferred_element_type=F32)


def _inproj_call(x, shift, scale, gain, w_in_bf, n_xg, per_batch_mod):
    b, t, d = x.shape
    n_all = w_in_bf.shape[1]
    tt = min(t, 512)
    mod_map = (lambda bi, i: (bi, 0, 0)) if per_batch_mod else (lambda bi, i: (0, 0, 0))
    return pl.pallas_call(
        functools.partial(_inproj_kernel, n_xg=n_xg),
        grid=(b, t // tt),
        in_specs=[
            pl.BlockSpec((1, tt, d), lambda bi, i: (bi, i, 0)),
            pl.BlockSpec((1, 1, d), mod_map),
            pl.BlockSpec((1, 1, d), mod_map),
            pl.BlockSpec((1, d), lambda bi, i: (0, 0)),
            pl.BlockSpec((d, n_all), lambda bi, i: (0, 0)),
        ],
        out_specs=[
            pl.BlockSpec((1, tt, n_xg), lambda bi, i: (bi, i, 0)),
            pl.BlockSpec((1, tt, n_all - n_xg), lambda bi, i: (bi, i, 0)),
        ],
        out_shape=[
            jax.ShapeDtypeStruct((b, t, n_xg), F32),
            jax.ShapeDtypeStruct((b, t, n_all - n_xg), F32),
        ],
        compiler_params=_cparams("parallel", "arbitrary"),
        name="inproj",
    )(x, shift, scale, gain, w_in_bf)


SCAN_ROWS = 64


def _tile_scan(a, b, reverse):
    row = lax.broadcasted_iota(jnp.int32, a.shape, 0)
    for s in (1, 2, 4):
        if reverse:
            a_s = pltpu.roll(a, SUBLANES - s, 0)
            b_s = pltpu.roll(b, SUBLANES - s, 0)
            m = row < SUBLANES - s
        else:
            a_s = pltpu.roll(a, s, 0)
            b_s = pltpu.roll(b, s, 0)
            m = row >= s
        b = jnp.where(m, a * b_s + b, b)
        a = jnp.where(m, a * a_s, a)
    return a, b


def _lru_kernel(xa_ref, ga_ref, xac_ref, cw_ref, cb_ref, wg_ref, bg_ref, lam_ref, y_ref,
                hf_ref, xe_ref, a_ref, b_ref, carry_ref, *, t_len, c_len, tc, nc):
    j = pl.program_id(1)
    half = wg_ref.shape[2]

    def gates(n, dirn):
        u = cb_ref[...]
        for k in range(cw_ref.shape[0]):
            u = u + cw_ref[k:k + 1, :] * xe_ref[6 + k:6 + k + n, :]
        lam = -lam_ref[dirn:dirn + 1, :]
        sp = jnp.maximum(lam, 0.0) + jnp.log1p(jnp.exp(-jnp.abs(lam)))
        for hf in range(2):
            uh = u[:, hf * half:(hf + 1) * half]
            z = jnp.dot(uh.astype(BF16), wg_ref[dirn, hf], preferred_element_type=F32) + bg_ref[dirn, hf]
            r = jax.nn.sigmoid(z[:, :half])
            i = jax.nn.sigmoid(z[:, half:])
            log_a = -LRU_C * r * sp[:, hf * half:(hf + 1) * half]
            a = jnp.exp(log_a)
            a_ref[0:n, hf * half:(hf + 1) * half] = a
            b_ref[0:n, hf * half:(hf + 1) * half] = jnp.sqrt((1.0 - a) * (1.0 + a)) * (i * uh)

    def scan(n, dirn, emit):
        ng = n // SCAN_ROWS
        nt = SCAN_ROWS // SUBLANES

        def body(gi, carry):
            g = gi if dirn == 0 else ng - 1 - gi
            base = pl.multiple_of(g * SCAN_ROWS, SCAN_ROWS)
            av = a_ref[pl.ds(base, SCAN_ROWS), :]
            bv = b_ref[pl.ds(base, SCAN_ROWS), :]
            hs = [None] * nt
            order = range(nt) if dirn == 0 else range(nt - 1, -1, -1)
            for k in order:
                a, b = _tile_scan(av[k * SUBLANES:(k + 1) * SUBLANES], bv[k * SUBLANES:(k + 1) * SUBLANES],
                                  dirn == 1)
                h = a * carry + b
                carry = h[SUBLANES - 1:SUBLANES] if dirn == 0 else h[0:1]
                hs[k] = h
            if emit is not None:
                emit(base, jnp.concatenate(hs, axis=0))
            return carry

        carry_ref[...] = lax.fori_loop(0, ng, body, carry_ref[...])

    def run(dirn):
        first = (j == 0) if dirn == 0 else (j == nc)
        ci = j if dirn == 0 else 2 * nc - 1 - j

        @pl.when(first)
        def _():
            zeros = jnp.zeros((SUBLANES, xe_ref.shape[1]), F32)
            xe_ref[0:SUBLANES] = zeros
            xe_ref[SUBLANES:SUBLANES + c_len] = xac_ref[0]
            xe_ref[SUBLANES + c_len:2 * SUBLANES + c_len] = zeros
            gates(c_len, dirn)
            carry_ref[...] = jnp.zeros(carry_ref.shape, F32)
            scan(c_len, dirn, None)

        r0 = pl.multiple_of(ci * tc, tc)
        p0 = pl.multiple_of(jnp.maximum(r0 - SUBLANES, 0), SUBLANES)
        n0 = pl.multiple_of(jnp.minimum(r0 + tc, t_len - SUBLANES), SUBLANES)
        prev = xa_ref[0, pl.ds(p0, SUBLANES), :]
        nxt = xa_ref[0, pl.ds(n0, SUBLANES), :]
        xe_ref[0:SUBLANES] = jnp.where(ci > 0, prev, 0.0)
        xe_ref[SUBLANES:SUBLANES + tc] = xa_ref[0, pl.ds(r0, tc), :]
        xe_ref[SUBLANES + tc:2 * SUBLANES + tc] = jnp.where(ci < nc - 1, nxt, 0.0)
        gates(tc, dirn)

        if dirn == 0:
            def emit(base, h):
                hf_ref[pl.ds(r0 + base, SCAN_ROWS), :] = h
        else:
            def emit(base, h):
                tot = hf_ref[pl.ds(r0 + base, SCAN_ROWS), :] + h
                ga = ga_ref[0, pl.ds(base, SCAN_ROWS), :]
                y_ref[0, pl.ds(base, SCAN_ROWS), :] = (tot * jax.nn.gelu(ga)).astype(y_ref.dtype)
        scan(tc, dirn, emit)

    @pl.when(j < nc)
    def _():
        run(0)

    @pl.when(j >= nc)
    def _():
        run(1)


def _lru_call(xg, xg_c, conv_w, conv_b, wg, bg, lam):
    b, t, _ = xg.shape
    c_len = xg_c.shape[1]
    d_lru = conv_w.shape[1]
    tc = min(t, 256)
    nc = t // tc
    assert c_len <= tc and c_len % SCAN_ROWS == 0 and tc % SCAN_ROWS == 0

    def chunk_map(bi, j):
        return (bi, jnp.where(j < nc, nc - 1, 2 * nc - 1 - j), 0)

    def ga_map(bi, j):
        return (bi, jnp.where(j < nc, nc - 1, 2 * nc - 1 - j), 1)

    return pl.pallas_call(
        functools.partial(_lru_kernel, t_len=t, c_len=c_len, tc=tc, nc=nc),
        grid=(b, 2 * nc),
        in_specs=[
            pl.BlockSpec((1, t, d_lru), lambda bi, j: (bi, 0, 0)),
            pl.BlockSpec((1, tc, d_lru), ga_map),
            pl.BlockSpec((1, c_len, d_lru), lambda bi, j: (bi, 0, 0)),
            pl.BlockSpec(conv_w.shape, lambda bi, j: (0, 0)),
            pl.BlockSpec((1, d_lru), lambda bi, j: (0, 0)),
            pl.BlockSpec(wg.shape, lambda bi, j: (0, 0, 0, 0)),
            pl.BlockSpec(bg.shape, lambda bi, j: (0, 0, 0, 0)),
            pl.BlockSpec(lam.shape, lambda bi, j: (0, 0)),
        ],
        out_specs=pl.BlockSpec((1, tc, d_lru), chunk_map),
        out_shape=jax.ShapeDtypeStruct((b, t, d_lru), BF16),
        scratch_shapes=[
            pltpu.VMEM((t, d_lru), F32),
            pltpu.VMEM((tc + 2 * SUBLANES, d_lru), F32),
            pltpu.VMEM((tc, d_lru), F32),
            pltpu.VMEM((tc, d_lru), F32),
            pltpu.VMEM((1, d_lru), F32),
        ],
        compiler_params=_cparams("parallel", "arbitrary"),
        name="lru",
    )(xg, xg, xg_c, conv_w, conv_b.reshape(1, d_lru), wg, bg, lam)


def _ret_kernel(q_ref, k_ref, v_ref, g_ref, kc_ref, vc_ref, cos_ref, sin_ref, dm_ref, rt_ref, cwt_ref, gc_ref,
                y_ref, qr_ref, kr_ref, sf_ref, sb_ref, *, nc, scale):
    C = RET_CHUNK
    dh = q_ref.shape[2]
    lane = lax.broadcasted_iota(jnp.int32, (C, dh), 1)
    low = (lane % (dh // 2)) < (dh // 4)

    def rope(xv, cs, sn):
        partner = jnp.where(low, pltpu.roll(xv, dh - dh // 4, 1), pltpu.roll(xv, dh // 4, 1))
        return xv * cs + partner * sn

    xi_f, xi_b, ze_f, ze_b = rt_ref[0, 0], rt_ref[0, 1], rt_ref[0, 2], rt_ref[0, 3]
    g_chunk = gc_ref[0]
    g_full = jnp.concatenate([g_chunk] * (dh // SUBLANES), axis=0)

    kc = (kc_ref[0] * scale).T.astype(BF16)
    vc = vc_ref[0]
    s_f0 = jnp.dot(kc, (vc * cwt_ref[0, 0]).astype(BF16), preferred_element_type=F32)
    s_b0 = jnp.dot(kc, (vc * cwt_ref[0, 1]).astype(BF16), preferred_element_type=F32)

    def p1(n, s):
        r = pl.multiple_of(n * C, C)
        cs, sn = cos_ref[pl.ds(r, C), :], sin_ref[pl.ds(r, C), :]
        k = rope(k_ref[0, pl.ds(r, C), :], cs, sn) * scale
        qr_ref[pl.ds(r, C), :] = rope(q_ref[0, pl.ds(r, C), :], cs, sn)
        kr_ref[pl.ds(r, C), :] = k.astype(BF16)
        v = v_ref[0, pl.ds(r, C), :]
        vz = jnp.concatenate([v * ze_f, v * ze_b], axis=1).astype(BF16)
        kv = jnp.dot(k.T.astype(BF16), vz, preferred_element_type=F32)
        sf_ref[n] = s
        sb_ref[n] = kv[:, dh:]
        return g_full * s + kv[:, :dh]

    lax.fori_loop(0, nc, p1, s_f0)

    def p1b(i, s):
        n = nc - 1 - i
        kvb = sb_ref[n]
        sb_ref[n] = s
        return g_full * s + kvb

    lax.fori_loop(0, nc, p1b, s_b0)

    def p2(n, carry):
        r = pl.multiple_of(n * C, C)
        q = qr_ref[pl.ds(r, C), :]
        kb = kr_ref[pl.ds(r, C), :]
        vb = v_ref[0, pl.ds(r, C), :].astype(BF16)
        sc = lax.dot_general(q.astype(BF16), kb, NT_DIMS, preferred_element_type=F32) * dm_ref[0]
        o = jnp.dot(sc.astype(BF16), vb, preferred_element_type=F32)
        qx = jnp.concatenate([q * xi_f, q * xi_b], axis=1).astype(BF16)
        st = jnp.concatenate([sf_ref[n], sb_ref[n]], axis=0).astype(BF16)
        o = o + jnp.dot(qx, st, preferred_element_type=F32)
        o = o * lax.rsqrt(jnp.mean(o * o, axis=-1, keepdims=True) + EPS)
        y_ref[0, pl.ds(r, C), :] = (o * _silu(g_ref[0, pl.ds(r, C), :])).astype(y_ref.dtype)
        return carry

    lax.fori_loop(0, nc, p2, 0)


def _ret_tables(t, c_len, dh):
    C = RET_CHUNK
    hh = jnp.arange(RET_HEADS, dtype=F32)
    log_g = jnp.log1p(-jnp.exp2(-(5.0 + 2.0 * hh)))
    idx = jnp.arange(C, dtype=F32)
    dm = jnp.exp(jnp.abs(idx[:, None] - idx[None, :])[None] * log_g[:, None, None])

    def lanes(e):
        return jnp.broadcast_to(jnp.exp(e[None, :] * log_g[:, None])[:, :, None], (RET_HEADS, e.shape[0], dh))

    rt = jnp.stack([lanes(idx + 1.0), lanes(C - idx), lanes(C - 1.0 - idx), lanes(idx)], axis=1)
    pos = jnp.arange(c_len, dtype=F32)
    cwt = jnp.stack([lanes(c_len - 1.0 - pos), lanes(pos)], axis=1)
    gc = jnp.broadcast_to(jnp.exp(C * log_g)[:, None, None], (RET_HEADS, SUBLANES, dh))
    rows = t // GRID_W
    row = jnp.repeat(jnp.arange(rows, dtype=F32), GRID_W)
    col = jnp.tile(jnp.arange(GRID_W, dtype=F32), rows)
    q4 = dh // 4
    freqs = ROPE_BASE ** (-jnp.arange(q4, dtype=F32) / q4)
    ang_r = row[:, None] * freqs[None, :]
    ang_c = col[:, None] * freqs[None, :]
    cos = jnp.concatenate([jnp.cos(ang_r)] * 2 + [jnp.cos(ang_c)] * 2, axis=1)
    sin = jnp.concatenate([-jnp.sin(ang_r), jnp.sin(ang_r), -jnp.sin(ang_c), jnp.sin(ang_c)], axis=1)
    return dm, rt, cwt, gc, cos, sin


def _ret_call(qkvg, qkvg_c):
    b, t, w = qkvg.shape
    c_len = qkvg_c.shape[1]
    dh = w // (4 * RET_HEADS)
    nc = t // RET_CHUNK
    dm, rt, cwt, gc, cos, sin = _ret_tables(t, c_len, dh)
    H = RET_HEADS

    def col(off):
        return lambda bi, h: (bi, 0, off + h)

    return pl.pallas_call(
        functools.partial(_ret_kernel, nc=nc, scale=dh ** -0.5),
        grid=(b, H),
        in_specs=[
            pl.BlockSpec((1, t, dh), col(0)),
            pl.BlockSpec((1, t, dh), col(H)),
            pl.BlockSpec((1, t, dh), col(2 * H)),
            pl.BlockSpec((1, t, dh), col(3 * H)),
            pl.BlockSpec((1, c_len, dh), col(H)),
            pl.BlockSpec((1, c_len, dh), col(2 * H)),
            pl.BlockSpec((t, dh), lambda bi, h: (0, 0)),
            pl.BlockSpec((t, dh), lambda bi, h: (0, 0)),
            pl.BlockSpec((1, RET_CHUNK, RET_CHUNK), lambda bi, h: (h, 0, 0)),
            pl.BlockSpec((1, 4, RET_CHUNK, dh), lambda bi, h: (h, 0, 0, 0)),
            pl.BlockSpec((1, 2, c_len, dh), lambda bi, h: (h, 0, 0, 0)),
            pl.BlockSpec((1, SUBLANES, dh), lambda bi, h: (h, 0, 0)),
        ],
        out_specs=pl.BlockSpec((1, t, dh), col(0)),
        out_shape=jax.ShapeDtypeStruct((b, t, H * dh), BF16),
        scratch_shapes=[
            pltpu.VMEM((t, dh), F32),
            pltpu.VMEM((t, dh), BF16),
            pltpu.VMEM((nc, dh, dh), F32),
            pltpu.VMEM((nc, dh, dh), F32),
        ],
        compiler_params=_cparams("parallel", "arbitrary"),
        name="retention",
    )(qkvg, qkvg, qkvg, qkvg, qkvg_c, qkvg_c, cos, sin, dm, rt, cwt, gc)


def _route(logits_t, bias):
    e, n = logits_t.shape
    per = e // N_GROUPS
    s = jax.nn.sigmoid(logits_t)
    sb = s + bias
    neg = -jnp.inf
    g3 = sb.reshape(N_GROUPS, per, n)
    it3 = lax.broadcasted_iota(jnp.int32, g3.shape, 1)
    m1 = jnp.max(g3, axis=1, keepdims=True)
    i1 = jnp.min(jnp.where(g3 == m1, it3, per), axis=1, keepdims=True)
    m2 = jnp.max(jnp.where(it3 == i1, neg, g3), axis=1, keepdims=True)
    gs = (m1 + m2).reshape(N_GROUPS, n)
    itg = lax.broadcasted_iota(jnp.int32, gs.shape, 0)
    gsel = jnp.zeros(gs.shape, jnp.bool_)
    for _ in range(TOP_GROUPS):
        m = jnp.max(gs, axis=0, keepdims=True)
        i = jnp.min(jnp.where(gs == m, itg, N_GROUPS), axis=0, keepdims=True)
        hit = itg == i
        gsel = gsel | hit
        gs = jnp.where(hit, neg, gs)
    gmask = jnp.broadcast_to(gsel.reshape(N_GROUPS, 1, n), (N_GROUPS, per, n)).reshape(e, n)
    v = jnp.where(gmask, sb, neg)
    ite = lax.broadcasted_iota(jnp.int32, v.shape, 0)
    sel = jnp.zeros(v.shape, jnp.bool_)
    for _ in range(TOP_K):
        m = jnp.max(v, axis=0, keepdims=True)
        i = jnp.min(jnp.where(v == m, ite, e), axis=0, keepdims=True)
        hit = ite == i
        sel = sel | hit
        v = jnp.where(hit, neg, v)
    picked = jnp.where(sel, s, 0.0)
    return picked / jnp.sum(picked, axis=0, keepdims=True) * ROUTE_SCALE


def _outproj_kernel(yl_ref, yr_ref, x_ref, g1_ref, sh_ref, sc_ref, nf_ref, wo_ref, wrt_ref, rb_ref,
                    x1_ref, h2_ref, gt_ref, cnt_ref):
    dl = yl_ref.shape[2]
    mx = jnp.dot(yl_ref[0], wo_ref[0:dl, :], preferred_element_type=F32)
    mx = mx + jnp.dot(yr_ref[0], wo_ref[dl:, :], preferred_element_type=F32)
    x1 = x_ref[0] + g1_ref[0] * mx
    x1_ref[0] = x1
    h = _rms_mod(x1, nf_ref[...], sh_ref[0], sc_ref[0])
    h2_ref[0] = h.astype(h2_ref.dtype)
    logits_t = lax.dot_general(wrt_ref[...], h, NT_DIMS, precision=HIGHEST, preferred_element_type=F32)
    gates = _route(logits_t, rb_ref[...])
    gt_ref[...] = gates
    hit = (gates > 0.0).astype(BF16)
    cnt_ref[0] = lax.dot_general(jnp.ones((SUBLANES, hit.shape[1]), BF16), hit, NT_DIMS, preferred_element_type=F32)


def _outproj_call(y_lru, y_ret, x, g1, sh2, sc2, norm_ffn, w_out_bf, w_router_t, b_router):
    b, t, d = x.shape
    dl = y_lru.shape[2]
    e = w_router_t.shape[0]
    tt = min(t, MOE_TILE)
    nt = t // tt
    mod_map = lambda bi, i: (bi, 0, 0)
    return pl.pallas_call(
        _outproj_kernel,
        grid=(b, nt),
        in_specs=[
            pl.BlockSpec((1, tt, dl), lambda bi, i: (bi, i, 0)),
            pl.BlockSpec((1, tt, d - dl), lambda bi, i: (bi, i, 0)),
            pl.BlockSpec((1, tt, d), lambda bi, i: (bi, i, 0)),
            pl.BlockSpec((1, 1, d), mod_map),
            pl.BlockSpec((1, 1, d), mod_map),
            pl.BlockSpec((1, 1, d), mod_map),
            pl.BlockSpec((1, d), lambda bi, i: (0, 0)),
            pl.BlockSpec((d, d), lambda bi, i: (0, 0)),
            pl.BlockSpec((e, d), lambda bi, i: (0, 0)),
            pl.BlockSpec((e, 1), lambda bi, i: (0, 0)),
        ],
        out_specs=[
            pl.BlockSpec((1, tt, d), lambda bi, i: (bi, i, 0)),
            pl.BlockSpec((1, tt, d), lambda bi, i: (bi, i, 0)),
            pl.BlockSpec((e, tt), lambda bi, i: (0, bi * nt + i)),
            pl.BlockSpec((1, SUBLANES, e), lambda bi, i: (bi * nt + i, 0, 0)),
        ],
        out_shape=[
            jax.ShapeDtypeStruct((b, t, d), F32),
            jax.ShapeDtypeStruct((b, t, d), BF16),
            jax.ShapeDtypeStruct((e, b * t), F32),
            jax.ShapeDtypeStruct((b * nt, SUBLANES, e), F32),
        ],
        compiler_params=_cparams("parallel", "arbitrary"),
        name="outproj_route",
    )(y_lru, y_ret, x, g1, sh2, sc2, norm_ffn.reshape(1, d), w_out_bf, w_router_t, b_router.reshape(e, 1))


MOE_TILE = 256
MOE_CHUNK = 16
MOE_RB = 256
MOE_BM = 512
NO_RANK = 512.0

TN_DIMS = (((0,), (0,)), ((), ()))
TAB_LOFF, TAB_GBASE, TAB_CPAD, TAB_NROWS, TAB_EEND = range(5)


def _round_up(v, m):
    return -(-v // m) * m


def _tables_kernel(cnt_ref, loff_ref, gbase_ref, cpad_ref, eend_ref, blk_ref, nused_ref, *, nb):
    cnt = cnt_ref[...]
    ns, ne = cnt.shape
    cpad = jnp.ceil(cnt * (1.0 / MOE_CHUNK)) * MOE_CHUNK
    upper = (lax.broadcasted_iota(jnp.int32, (ne, ne), 0) < lax.broadcasted_iota(jnp.int32, (ne, ne), 1)).astype(F32)
    lower = (lax.broadcasted_iota(jnp.int32, (ns, ns), 1) < lax.broadcasted_iota(jnp.int32, (ns, ns), 0)).astype(F32)
    loff = jnp.dot(cpad, upper, precision=HIGHEST, preferred_element_type=F32)
    before = jnp.dot(lower, cpad, precision=HIGHEST, preferred_element_type=F32)
    tot = jnp.sum(cpad, axis=0, keepdims=True)
    epad = jnp.ceil(tot * (1.0 / MOE_BM)) * MOE_BM
    epad8 = jnp.broadcast_to(epad, (SUBLANES, ne))
    ebase = jnp.dot(epad8, upper, precision=HIGHEST, preferred_element_type=F32)[0:1]
    eend = ebase + epad
    loff_ref[...] = loff
    gbase_ref[...] = ebase + before
    cpad_ref[...] = cpad
    eend_ref[...] = eend
    start = lax.broadcasted_iota(jnp.int32, (nb, ne), 0).astype(F32) * MOE_BM
    blk = jnp.sum((eend <= start).astype(F32), axis=1, keepdims=True)
    blk_ref[...] = jnp.minimum(blk, ne - 1.0)
    nused_ref[...] = jnp.sum(epad, axis=1, keepdims=True) * (1.0 / MOE_BM)


def _tables_call(cnt, nb):
    ns, ne = cnt.shape
    f = lambda shape: jax.ShapeDtypeStruct(shape, F32)
    return pl.pallas_call(
        functools.partial(_tables_kernel, nb=nb),
        out_shape=[f((ns, ne)), f((ns, ne)), f((ns, ne)), f((1, ne)), f((nb, 1)), f((1, 1))],
        compiler_params=pltpu.CompilerParams(vmem_limit_bytes=VMEM_LIMIT_BYTES),
        name="moe_tables",
    )(cnt)


def _tile_rank_rhs(gt, lcol):
    ne, ts = gt.shape
    hit = gt > 0.0
    before = (lax.broadcasted_iota(jnp.int32, (ts, ts), 0) < lax.broadcasted_iota(jnp.int32, (ts, ts), 1)).astype(BF16)
    rank = jnp.dot(hit.astype(BF16), before, preferred_element_type=F32)
    rm = jnp.where(hit, rank, NO_RANK)
    q = jnp.broadcast_to(lcol * (1.0 / MOE_CHUNK), (ne, LANES))
    return jnp.concatenate([rm, q], axis=1).astype(BF16)


def _onehot_rows(r0, rhs, lrow, crow, ts):
    ne = lrow.shape[1]
    rows = (r0 + lax.broadcasted_iota(jnp.int32, (MOE_RB, ne), 0)).astype(F32)
    seg = ((rows >= lrow) & (rows < crow)).astype(BF16)
    ex = jnp.dot(seg, rhs, preferred_element_type=F32)
    rowl = (r0 + lax.broadcasted_iota(jnp.int32, (MOE_RB, LANES), 0)).astype(F32)
    tgt = rowl - MOE_CHUNK * ex[:, ts:ts + LANES]
    p = ex[:, :ts] == jnp.concatenate([tgt] * (ts // LANES), axis=1)
    return p, (ex[:, ts + LANES:] if rhs.shape[1] > ts + LANES else None)


def _chunk_copies(tab_ref, buf, slot, hbm, sem, to_hbm):
    ne = tab_ref.shape[2]

    def per_expert(e, c):
        lo = tab_ref[0, TAB_LOFF, e]
        gb = tab_ref[0, TAB_GBASE, e]
        nch = tab_ref[0, TAB_CPAD, e] // MOE_CHUNK

        def per_chunk(j, c2):
            loc = buf.at[slot, pl.ds(pl.multiple_of(lo + j * MOE_CHUNK, MOE_CHUNK), MOE_CHUNK)]
            glob = hbm.at[pl.ds(pl.multiple_of(gb + j * MOE_CHUNK, MOE_CHUNK), MOE_CHUNK)]
            if to_hbm:
                pltpu.make_async_copy(loc, glob, sem.at[slot]).start()
            else:
                pltpu.make_async_copy(glob, loc, sem.at[slot]).start()
            return c2

        lax.fori_loop(0, nch, per_chunk, 0)
        return c

    lax.fori_loop(0, ne, per_expert, 0)


def _wait_chunks(n, buf, slot, hbm, sem, to_hbm):
    def body(i, c):
        loc = buf.at[slot, pl.ds(0, MOE_CHUNK)]
        glob = hbm.at[pl.ds(0, MOE_CHUNK)]
        if to_hbm:
            pltpu.make_async_copy(loc, glob, sem.at[slot]).wait()
        else:
            pltpu.make_async_copy(glob, loc, sem.at[slot]).wait()
        return c

    lax.fori_loop(0, n, body, 0)


def _dispatch_kernel(tab_ref, h_ref, gt_ref, lrow_ref, crow_ref, lcol_ref, xs_hbm, xbuf, zbuf, sem, pend):
    s = pl.program_id(0)
    ns = pl.num_programs(0)
    slot = s % 2
    ts = h_ref.shape[0]
    ne = gt_ref.shape[0]

    @pl.when(s == 0)
    def _():
        pend[0] = 0
        pend[1] = 0

    _wait_chunks(pend[slot], xbuf, slot, xs_hbm, sem, True)

    rhs = _tile_rank_rhs(gt_ref[...], lcol_ref[0])
    nrows = tab_ref[0, TAB_NROWS, 0]

    def block(rb, c):
        r0 = pl.multiple_of(rb * MOE_RB, MOE_RB)
        p, _ = _onehot_rows(r0, rhs, lrow_ref[0], crow_ref[0], ts)
        xbuf[slot, pl.ds(r0, MOE_RB), :] = jnp.dot(p.astype(BF16), h_ref[...],
                                                   preferred_element_type=F32).astype(xbuf.dtype)
        return c

    lax.fori_loop(0, (nrows + MOE_RB - 1) // MOE_RB, block, 0)
    _chunk_copies(tab_ref, xbuf, slot, xs_hbm, sem, True)
    pend[slot] = nrows // MOE_CHUNK

    @pl.when(s == ns - 1)
    def _():
        zbuf[...] = jnp.zeros(zbuf.shape, zbuf.dtype)

        def per_expert(e, total):
            first = tab_ref[0, TAB_GBASE, e] + tab_ref[0, TAB_CPAD, e]
            nch = (tab_ref[0, TAB_EEND, e] - first) // MOE_CHUNK

            def per_chunk(j, c2):
                dst = xs_hbm.at[pl.ds(pl.multiple_of(first + j * MOE_CHUNK, MOE_CHUNK), MOE_CHUNK)]
                pltpu.make_async_copy(zbuf, dst, sem.at[slot]).start()
                return c2

            lax.fori_loop(0, nch, per_chunk, 0)
            return total + nch

        ntail = lax.fori_loop(0, ne, per_expert, 0)
        _wait_chunks(pend[slot] + ntail, xbuf, slot, xs_hbm, sem, True)
        _wait_chunks(pend[1 - slot], xbuf, 1 - slot, xs_hbm, sem, True)


def _dispatch_call(tab, h2, gt, lrow, crow, lcol, p_max, rows_max):
    n, d = h2.shape
    ne = gt.shape[0]
    ts = min(n, MOE_TILE)
    ns = n // ts
    return pl.pallas_call(
        _dispatch_kernel,
        grid=(ns,),
        in_specs=[
            pl.BlockSpec((1, 5, ne), lambda s: (s, 0, 0), memory_space=pltpu.SMEM),
            pl.BlockSpec((ts, d), lambda s: (s, 0)),
            pl.BlockSpec((ne, ts), lambda s: (0, s)),
            pl.BlockSpec((1, 1, ne), lambda s: (s, 0, 0)),
            pl.BlockSpec((1, 1, ne), lambda s: (s, 0, 0)),
            pl.BlockSpec((1, ne, 1), lambda s: (s, 0, 0)),
        ],
        out_specs=pl.BlockSpec(memory_space=pl.ANY),
        out_shape=jax.ShapeDtypeStruct((p_max, d), BF16),
        scratch_shapes=[
            pltpu.VMEM((2, rows_max, d), BF16),
            pltpu.VMEM((MOE_CHUNK, d), BF16),
            pltpu.SemaphoreType.DMA((2,)),
            pltpu.SMEM((2,), jnp.int32),
        ],
        compiler_params=_cparams("arbitrary"),
        name="moe_dispatch",
    )(tab, h2, gt, lrow, crow, lcol)


def _experts_kernel(blk_ref, nused_ref, x_ref, w1_ref, w3_ref, w2_ref, y_ref):
    del blk_ref

    @pl.when(pl.program_id(0) < nused_ref[0])
    def _():
        xv = x_ref[...]
        hid = _silu(jnp.dot(xv, w1_ref[0], preferred_element_type=F32)) * jnp.dot(xv, w3_ref[0], preferred_element_type=F32)
        y_ref[...] = jnp.dot(hid.astype(BF16), w2_ref[0], preferred_element_type=F32).astype(y_ref.dtype)


def _experts_call(blk, nused, xs, w1, w3, w2):
    p_max, d = xs.shape
    _, _, de = w1.shape
    nb = p_max // MOE_BM
    row_map = lambda i, blk_r, nu_r: (jnp.minimum(i, nu_r[0] - 1), 0)
    w_map = lambda i, blk_r, nu_r: (blk_r[i], 0, 0)
    return pl.pallas_call(
        _experts_kernel,
        grid_spec=pltpu.PrefetchScalarGridSpec(
            num_scalar_prefetch=2,
            grid=(nb,),
            in_specs=[
                pl.BlockSpec((MOE_BM, d), row_map),
                pl.BlockSpec((1, d, de), w_map),
                pl.BlockSpec((1, d, de), w_map),
                pl.BlockSpec((1, de, d), w_map),
            ],
            out_specs=pl.BlockSpec((MOE_BM, d), row_map),
        ),
        out_shape=jax.ShapeDtypeStruct((p_max, d), BF16),
        compiler_params=_cparams("arbitrary"),
        name="moe_experts",
    )(blk, nused, xs, w1, w3, w2)


def _combine_kernel(tab_ref, tabn_ref, gt_ref, lrow_ref, crow_ref, lcol_ref, h_ref, x1_ref, g2_ref,
                    s1_ref, s3_ref, s2_ref, nfin_ref, ys_hbm, o_ref, ybuf, acc_ref, sem):
    s = pl.program_id(0)
    ns = pl.num_programs(0)
    slot = s % 2
    ts = h_ref.shape[0]

    @pl.when(s == 0)
    def _():
        ybuf[...] = jnp.zeros(ybuf.shape, ybuf.dtype)
        _chunk_copies(tab_ref, ybuf, 0, ys_hbm, sem, False)

    @pl.when(s + 1 < ns)
    def _():
        _chunk_copies(tabn_ref, ybuf, 1 - slot, ys_hbm, sem, False)

    h = h_ref[...]
    hs = _silu(jnp.dot(h, s1_ref[...], preferred_element_type=F32)) * jnp.dot(h, s3_ref[...], preferred_element_type=F32)
    acc_ref[...] = jnp.dot(hs.astype(BF16), s2_ref[...], preferred_element_type=F32)

    gt = gt_ref[...]
    rhs = jnp.concatenate([_tile_rank_rhs(gt, lcol_ref[0]), gt.astype(BF16)], axis=1)
    nrows = tab_ref[0, TAB_NROWS, 0]
    _wait_chunks(nrows // MOE_CHUNK, ybuf, slot, ys_hbm, sem, False)

    def block(rb, c):
        r0 = pl.multiple_of(rb * MOE_RB, MOE_RB)
        p, gexp = _onehot_rows(r0, rhs, lrow_ref[0], crow_ref[0], ts)
        gm = jnp.where(p, gexp, 0.0).astype(BF16)
        acc_ref[...] += lax.dot_general(gm, ybuf[slot, pl.ds(r0, MOE_RB), :], TN_DIMS, preferred_element_type=F32)
        return c

    lax.fori_loop(0, (nrows + MOE_RB - 1) // MOE_RB, block, 0)
    xo = x1_ref[...] + g2_ref[0] * acc_ref[...]
    o_ref[...] = xo * lax.rsqrt(jnp.mean(xo * xo, axis=-1, keepdims=True) + EPS) * nfin_ref[...]


def _combine_call(tab, gt, lrow, crow, lcol, h2, x1, g2, ws1, ws3, ws2, norm_final, ys, rows_max, t):
    n, d = h2.shape
    ne = gt.shape[0]
    ts = min(n, MOE_TILE)
    ns = n // ts
    per_b = t // ts
    return pl.pallas_call(
        _combine_kernel,
        grid=(ns,),
        in_specs=[
            pl.BlockSpec((1, 5, ne), lambda s: (s, 0, 0), memory_space=pltpu.SMEM),
            pl.BlockSpec((1, 5, ne), lambda s: (jnp.minimum(s + 1, ns - 1), 0, 0), memory_space=pltpu.SMEM),
            pl.BlockSpec((ne, ts), lambda s: (0, s)),
            pl.BlockSpec((1, 1, ne), lambda s: (s, 0, 0)),
            pl.BlockSpec((1, 1, ne), lambda s: (s, 0, 0)),
            pl.BlockSpec((1, ne, 1), lambda s: (s, 0, 0)),
            pl.BlockSpec((ts, d), lambda s: (s, 0)),
            pl.BlockSpec((ts, d), lambda s: (s, 0)),
            pl.BlockSpec((1, 1, d), lambda s: (s // per_b, 0, 0)),
            pl.BlockSpec(ws1.shape, lambda s: (0, 0)),
            pl.BlockSpec(ws3.shape, lambda s: (0, 0)),
            pl.BlockSpec(ws2.shape, lambda s: (0, 0)),
            pl.BlockSpec((1, d), lambda s: (0, 0)),
            pl.BlockSpec(memory_space=pl.ANY),
        ],
        out_specs=pl.BlockSpec((ts, d), lambda s: (s, 0)),
        out_shape=jax.ShapeDtypeStruct((n, d), F32),
        scratch_shapes=[
            pltpu.VMEM((2, rows_max, d), BF16),
            pltpu.VMEM((ts, d), F32),
            pltpu.SemaphoreType.DMA((2,)),
        ],
        compiler_params=_cparams("arbitrary"),
        name="moe_combine",
    )(tab, tab, gt, lrow, crow, lcol, h2, x1, g2, ws1, ws3, ws2, norm_final.reshape(1, d), ys)


def _moe_call(h2, gt, cnt, x1, g2, w1, w3, w2, ws1, ws3, ws2, norm_final, t):
    n, d = h2.shape
    ne = gt.shape[0]
    ts = min(n, MOE_TILE)
    ns = n // ts
    rows_max = _round_up(TOP_K * ts + ne * (MOE_CHUNK - 1), MOE_RB)
    p_max = _round_up(n * TOP_K + ns * ne * (MOE_CHUNK - 1) + ne * (MOE_BM - 1), MOE_BM)
    nb = p_max // MOE_BM

    loff, gbase, cpad, eend, blk, nused = _tables_call(cnt[:, 0, :], nb)
    nrows = loff[:, ne - 1:] + cpad[:, ne - 1:]
    tab = jnp.stack([loff, gbase, cpad, jnp.broadcast_to(nrows, (ns, ne)), jnp.broadcast_to(eend, (ns, ne))],
                    axis=1).astype(jnp.int32)
    lrow = loff.reshape(ns, 1, ne)
    crow = (loff + cpad).reshape(ns, 1, ne)
    lcol = loff.reshape(ns, ne, 1)

    xs = _dispatch_call(tab, h2, gt, lrow, crow, lcol, p_max, rows_max)
    ys = _experts_call(blk.reshape(nb).astype(jnp.int32), nused.reshape(1).astype(jnp.int32), xs, w1, w3, w2)
    return _combine_call(tab, gt, lrow, crow, lcol, h2, x1, g2, ws1, ws3, ws2, norm_final, ys, rows_max, t)


def _block_diag(w):
    nb, n, _ = w.shape
    eye = jnp.eye(nb, dtype=w.dtype)
    return (eye[:, None, :, None] * w[:, :, None, :]).reshape(nb * n, nb * n)


def _lru_gate_weights(w_rec, b_rec, w_inp, b_inp):
    d_lru = b_rec.shape[1]
    half = d_lru // 2
    ws, bs = [], []
    for dirn in range(2):
        wr, wi = _block_diag(w_rec[dirn]), _block_diag(w_inp[dirn])
        ws.append(jnp.stack([jnp.concatenate([wr[s:s + half, s:s + half], wi[s:s + half, s:s + half]], axis=1)
                             for s in (0, half)]))
        bs.append(jnp.stack([jnp.concatenate([b_rec[dirn, s:s + half], b_inp[dirn, s:s + half]])[None, :]
                             for s in (0, half)]))
    return jnp.stack(ws).astype(BF16), jnp.stack(bs).astype(F32)


def kernel(x, c, ctx, c_ctx, w_mod, b_mod, norm_mix, norm_ffn, w_in, conv_w, conv_b, lru_w_rec, lru_b_rec, lru_w_in, lru_b_in, lru_lambda, w_out, w_router, b_router, w_exp_gate, w_exp_up, w_exp_down, w_sh_gate, w_sh_up, w_sh_down, norm_final):
    b, t, d = x.shape
    depth = w_mod.shape[0]
    assert depth == 1, "context-stream update for deeper stacks is not implemented"
    l = 0
    d_lru = conv_w.shape[2]
    n_xg = 2 * d_lru

    rows = -(-(b + 1) // SUBLANES) * SUBLANES
    cc = jnp.concatenate([c, c_ctx[None, :], jnp.zeros((rows - b - 1, d), F32)], axis=0)
    mod = _mod_call(cc, w_mod[l], b_mod[l]).reshape(rows, 1, -1)
    sh1, sc1, g1, sh2, sc2, g2 = (mod[:, :, i * d:(i + 1) * d] for i in range(6))

    w_in_bf = w_in[l].astype(BF16)
    gain_mix = norm_mix[l].reshape(1, d)
    xg, qkvg = _inproj_call(x, sh1[:b], sc1[:b], gain_mix, w_in_bf, n_xg, True)
    xg_c, qkvg_c = _inproj_call(ctx, sh1[b:b + 1], sc1[b:b + 1], gain_mix, w_in_bf, n_xg, False)

    wg, bg = _lru_gate_weights(lru_w_rec[l], lru_b_rec[l], lru_w_in[l], lru_b_in[l])
    y_lru = _lru_call(xg, xg_c, conv_w[l], conv_b[l], wg, bg, lru_lambda[l])
    y_ret = _ret_call(qkvg, qkvg_c)

    x1, h2, gt, cnt = _outproj_call(y_lru, y_ret, x, g1[:b], sh2[:b], sc2[:b], norm_ffn[l], w_out[l].astype(BF16),
                                    w_router[l].T, b_router[l])

    out = _moe_call(h2.reshape(b * t, d), gt, cnt, x1.reshape(b * t, d), g2[:b],
                    w_exp_gate[l].astype(BF16), w_exp_up[l].astype(BF16), w_exp_down[l].astype(BF16),
                    w_sh_gate[l].astype(BF16), w_sh_up[l].astype(BF16), w_sh_down[l].astype(BF16), norm_final, t)
    return out.reshape(b, t, d)
```

```python
import functools

import jax
import jax.numpy as jnp
from jax import lax
from jax.experimental import pallas as pl
from jax.experimental.pallas import tpu as pltpu

F32 = jnp.float32
BF16 = jnp.bfloat16
HIGHEST = lax.Precision.HIGHEST

EPS = 1e-6
GRID_W = 64
LRU_C = 8.0
RET_HEADS = 4
RET_CHUNK = 128
RET_UNROLL = 4
ROPE_BASE = 10000.0
N_GROUPS = 8
TOP_GROUPS = 4
TOP_K = 6
ROUTE_SCALE = 2.5

LANES = 128
SUBLANES = 8
VMEM_LIMIT_BYTES = 56 * 1024 * 1024

NT_DIMS = (((1,), (1,)), ((), ()))


def _cparams(*sem):
    return pltpu.CompilerParams(dimension_semantics=sem, vmem_limit_bytes=VMEM_LIMIT_BYTES)


def _silu(v):
    return v * jax.nn.sigmoid(v)


def _rms_mod(x, gain, shift, scale):
    y = x * lax.rsqrt(jnp.mean(x * x, axis=-1, keepdims=True) + EPS) * gain
    return y * (1.0 + scale) + shift


def _mod_kernel(c_ref, w_ref, b_ref, o_ref):
    s = _silu(c_ref[...])
    o_ref[...] = jnp.dot(s, w_ref[...], precision=HIGHEST, preferred_element_type=F32) + b_ref[...]


def _mod_call(cc, w_mod, b_mod):
    rows, d = cc.shape
    n = w_mod.shape[1]
    tn = 512
    return pl.pallas_call(
        _mod_kernel,
        grid=(n // tn,),
        in_specs=[
            pl.BlockSpec((rows, d), lambda j: (0, 0)),
            pl.BlockSpec((d, tn), lambda j: (0, j)),
            pl.BlockSpec((1, tn), lambda j: (0, j)),
        ],
        out_specs=pl.BlockSpec((rows, tn), lambda j: (0, j)),
        out_shape=jax.ShapeDtypeStruct((rows, n), F32),
        compiler_params=_cparams("arbitrary"),
        name="mod",
    )(cc, w_mod, b_mod.reshape(1, n))


def _inproj_kernel(x_ref, sh_ref, sc_ref, g_ref, w_ref, xg_ref, qkvg_ref, *, n_xg):
    h = _rms_mod(x_ref[0], g_ref[...], sh_ref[0], sc_ref[0]).astype(BF16)
    xg_ref[0] = jnp.dot(h, w_ref[:, :n_xg], preferred_element_type=F32)
    qkvg_ref[0] = jnp.dot(h, w_ref[:, n_xg:], preferred_element_type=F32)


def _inproj_call(x, shift, scale, gain, w_in_bf, n_xg, per_batch_mod):
    b, t, d = x.shape
    n_all = w_in_bf.shape[1]
    tt = min(t, 512)
    mod_map = (lambda bi, i: (bi, 0, 0)) if per_batch_mod else (lambda bi, i: (0, 0, 0))
    return pl.pallas_call(
        functools.partial(_inproj_kernel, n_xg=n_xg),
        grid=(b, t // tt),
        in_specs=[
            pl.BlockSpec((1, tt, d), lambda bi, i: (bi, i, 0)),
            pl.BlockSpec((1, 1, d), mod_map),
            pl.BlockSpec((1, 1, d), mod_map),
            pl.BlockSpec((1, d), lambda bi, i: (0, 0)),
            pl.BlockSpec((d, n_all), lambda bi, i: (0, 0)),
        ],
        out_specs=[
            pl.BlockSpec((1, tt, n_xg), lambda bi, i: (bi, i, 0)),
            pl.BlockSpec((1, tt, n_all - n_xg), lambda bi, i: (bi, i, 0)),
        ],
        out_shape=[
            jax.ShapeDtypeStruct((b, t, n_xg), F32),
            jax.ShapeDtypeStruct((b, t, n_all - n_xg), F32),
        ],
        compiler_params=_cparams("parallel", "arbitrary"),
        name="inproj",
    )(x, shift, scale, gain, w_in_bf)


SCAN_ROWS = 64


def _tile_scan(a, b, reverse):
    row = lax.broadcasted_iota(jnp.int32, a.shape, 0)
    for s in (1, 2, 4):
        if reverse:
            a_s = pltpu.roll(a, SUBLANES - s, 0)
            b_s = pltpu.roll(b, SUBLANES - s, 0)
            m = row < SUBLANES - s
        else:
            a_s = pltpu.roll(a, s, 0)
            b_s = pltpu.roll(b, s, 0)
            m = row >= s
        b = jnp.where(m, a * b_s + b, b)
        a = jnp.where(m, a * a_s, a)
    return a, b


def _lru_kernel(xa_ref, ga_ref, xac_ref, cw_ref, cb_ref, wg_ref, bg_ref, lam_ref, y_ref,
                hf_ref, xe_ref, a_ref, b_ref, carry_ref, *, t_len, c_len, tc, nc):
    j = pl.program_id(1)
    half = wg_ref.shape[2]

    def gates(n, dirn):
        u = cb_ref[...]
        for k in range(cw_ref.shape[0]):
            u = u + cw_ref[k:k + 1, :] * xe_ref[6 + k:6 + k + n, :]
        lam = -lam_ref[dirn:dirn + 1, :]
        sp = jnp.maximum(lam, 0.0) + jnp.log1p(jnp.exp(-jnp.abs(lam)))
        for hf in range(2):
            uh = u[:, hf * half:(hf + 1) * half]
            z = jnp.dot(uh.astype(BF16), wg_ref[dirn, hf], preferred_element_type=F32) + bg_ref[dirn, hf]
            r = jax.nn.sigmoid(z[:, :half])
            i = jax.nn.sigmoid(z[:, half:])
            log_a = -LRU_C * r * sp[:, hf * half:(hf + 1) * half]
            a = jnp.exp(log_a)
            a_ref[0:n, hf * half:(hf + 1) * half] = a
            b_ref[0:n, hf * half:(hf + 1) * half] = jnp.sqrt((1.0 - a) * (1.0 + a)) * (i * uh)

    def scan(n, dirn, emit):
        ng = n // SCAN_ROWS
        nt = SCAN_ROWS // SUBLANES

        def body(gi, carry):
            g = gi if dirn == 0 else ng - 1 - gi
            base = pl.multiple_of(g * SCAN_ROWS, SCAN_ROWS)
            av = a_ref[pl.ds(base, SCAN_ROWS), :]
            bv = b_ref[pl.ds(base, SCAN_ROWS), :]
            hs = [None] * nt
            order = range(nt) if dirn == 0 else range(nt - 1, -1, -1)
            for k in order:
                a, b = _tile_scan(av[k * SUBLANES:(k + 1) * SUBLANES], bv[k * SUBLANES:(k + 1) * SUBLANES],
                                  dirn == 1)
                h = a * carry + b
                carry = h[SUBLANES - 1:SUBLANES] if dirn == 0 else h[0:1]
                hs[k] = h
            if emit is not None:
                emit(base, jnp.concatenate(hs, axis=0))
            return carry

        carry_ref[...] = lax.fori_loop(0, ng, body, carry_ref[...])

    def run(dirn):
        first = (j == 0) if dirn == 0 else (j == nc)
        ci = j if dirn == 0 else 2 * nc - 1 - j

        @pl.when(first)
        def _():
            zeros = jnp.zeros((SUBLANES, xe_ref.shape[1]), F32)
            xe_ref[0:SUBLANES] = zeros
            xe_ref[SUBLANES:SUBLANES + c_len] = xac_ref[0]
            xe_ref[SUBLANES + c_len:2 * SUBLANES + c_len] = zeros
            gates(c_len, dirn)
            carry_ref[...] = jnp.zeros(carry_ref.shape, F32)
            scan(c_len, dirn, None)

        r0 = pl.multiple_of(ci * tc, tc)
        p0 = pl.multiple_of(jnp.maximum(r0 - SUBLANES, 0), SUBLANES)
        n0 = pl.multiple_of(jnp.minimum(r0 + tc, t_len - SUBLANES), SUBLANES)
        prev = xa_ref[0, pl.ds(p0, SUBLANES), :]
        nxt = xa_ref[0, pl.ds(n0, SUBLANES), :]
        xe_ref[0:SUBLANES] = jnp.where(ci > 0, prev, 0.0)
        xe_ref[SUBLANES:SUBLANES + tc] = xa_ref[0, pl.ds(r0, tc), :]
        xe_ref[SUBLANES + tc:2 * SUBLANES + tc] = jnp.where(ci < nc - 1, nxt, 0.0)
        gates(tc, dirn)

        if dirn == 0:
            def emit(base, h):
                hf_ref[pl.ds(r0 + base, SCAN_ROWS), :] = h
        else:
            def emit(base, h):
                tot = hf_ref[pl.ds(r0 + base, SCAN_ROWS), :] + h
                ga = ga_ref[0, pl.ds(base, SCAN_ROWS), :]
                y_ref[0, pl.ds(base, SCAN_ROWS), :] = (tot * jax.nn.gelu(ga)).astype(y_ref.dtype)
        scan(tc, dirn, emit)

    @pl.when(j < nc)
    def _():
        run(0)

    @pl.when(j >= nc)
    def _():
        run(1)


def _lru_call(xg, xg_c, conv_w, conv_b, wg, bg, lam):
    b, t, _ = xg.shape
    c_len = xg_c.shape[1]
    d_lru = conv_w.shape[1]
    tc = min(t, 256)
    nc = t // tc
    assert c_len <= tc and c_len % SCAN_ROWS == 0 and tc % SCAN_ROWS == 0

    def chunk_map(bi, j):
        return (bi, jnp.where(j < nc, nc - 1, 2 * nc - 1 - j), 0)

    def ga_map(bi, j):
        return (bi, jnp.where(j < nc, nc - 1, 2 * nc - 1 - j), 1)

    return pl.pallas_call(
        functools.partial(_lru_kernel, t_len=t, c_len=c_len, tc=tc, nc=nc),
        grid=(b, 2 * nc),
        in_specs=[
            pl.BlockSpec((1, t, d_lru), lambda bi, j: (bi, 0, 0)),
            pl.BlockSpec((1, tc, d_lru), ga_map),
            pl.BlockSpec((1, c_len, d_lru), lambda bi, j: (bi, 0, 0)),
            pl.BlockSpec(conv_w.shape, lambda bi, j: (0, 0)),
            pl.BlockSpec((1, d_lru), lambda bi, j: (0, 0)),
            pl.BlockSpec(wg.shape, lambda bi, j: (0, 0, 0, 0)),
            pl.BlockSpec(bg.shape, lambda bi, j: (0, 0, 0, 0)),
            pl.BlockSpec(lam.shape, lambda bi, j: (0, 0)),
        ],
        out_specs=pl.BlockSpec((1, tc, d_lru), chunk_map),
        out_shape=jax.ShapeDtypeStruct((b, t, d_lru), BF16),
        scratch_shapes=[
            pltpu.VMEM((t, d_lru), F32),
            pltpu.VMEM((tc + 2 * SUBLANES, d_lru), F32),
            pltpu.VMEM((tc, d_lru), F32),
            pltpu.VMEM((tc, d_lru), F32),
            pltpu.VMEM((1, d_lru), F32),
        ],
        compiler_params=_cparams("parallel", "arbitrary"),
        name="lru",
    )(xg, xg, xg_c, conv_w, conv_b.reshape(1, d_lru), wg, bg, lam)


def _ret_kernel(q_ref, k_ref, v_ref, g_ref, kc_ref, vc_ref, cos_ref, sin_ref, dm_ref, rt_ref, cwt_ref, gc_ref,
                y_ref, qr_ref, kr_ref, sf_ref, sb_ref, *, nc, scale):
    C = RET_CHUNK
    dh = q_ref.shape[2]
    lane = lax.broadcasted_iota(jnp.int32, (C, dh), 1)
    low = (lane % (dh // 2)) < (dh // 4)

    def rope(xv, cs, sn):
        partner = jnp.where(low, pltpu.roll(xv, dh - dh // 4, 1), pltpu.roll(xv, dh // 4, 1))
        return xv * cs + partner * sn

    xi_f, xi_b, ze_f, ze_b = rt_ref[0, 0], rt_ref[0, 1], rt_ref[0, 2], rt_ref[0, 3]
    g_chunk = gc_ref[0]
    g_full = jnp.concatenate([g_chunk] * (dh // SUBLANES), axis=0)

    kc = (kc_ref[0] * scale).T.astype(BF16)
    vc = vc_ref[0]
    s_f0 = jnp.dot(kc, (vc * cwt_ref[0, 0]).astype(BF16), preferred_element_type=F32)
    s_b0 = jnp.dot(kc, (vc * cwt_ref[0, 1]).astype(BF16), preferred_element_type=F32)

    def p1(n, s):
        r = pl.multiple_of(n * C, C)
        cs, sn = cos_ref[pl.ds(r, C), :], sin_ref[pl.ds(r, C), :]
        k = rope(k_ref[0, pl.ds(r, C), :], cs, sn) * scale
        qr_ref[pl.ds(r, C), :] = rope(q_ref[0, pl.ds(r, C), :], cs, sn)
        kr_ref[pl.ds(r, C), :] = k.astype(BF16)
        v = v_ref[0, pl.ds(r, C), :]
        vz = jnp.concatenate([v * ze_f, v * ze_b], axis=1).astype(BF16)
        kv = jnp.dot(k.T.astype(BF16), vz, preferred_element_type=F32)
        sf_ref[n] = s
        sb_ref[n] = kv[:, dh:]
        return g_full * s + kv[:, :dh]

    lax.fori_loop(0, nc, p1, s_f0, unroll=RET_UNROLL)

    def p1b(i, s):
        n = nc - 1 - i
        kvb = sb_ref[n]
        sb_ref[n] = s
        return g_full * s + kvb

    lax.fori_loop(0, nc, p1b, s_b0)

    def p2(n, carry):
        r = pl.multiple_of(n * C, C)
        q = qr_ref[pl.ds(r, C), :]
        kb = kr_ref[pl.ds(r, C), :]
        vb = v_ref[0, pl.ds(r, C), :].astype(BF16)
        sc = lax.dot_general(q.astype(BF16), kb, NT_DIMS, preferred_element_type=F32) * dm_ref[0]
        o = jnp.dot(sc.astype(BF16), vb, preferred_element_type=F32)
        qx = jnp.concatenate([q * xi_f, q * xi_b], axis=1).astype(BF16)
        st = jnp.concatenate([sf_ref[n], sb_ref[n]], axis=0).astype(BF16)
        o = o + jnp.dot(qx, st, preferred_element_type=F32)
        o = o * lax.rsqrt(jnp.mean(o * o, axis=-1, keepdims=True) + EPS)
        y_ref[0, pl.ds(r, C), :] = (o * _silu(g_ref[0, pl.ds(r, C), :])).astype(y_ref.dtype)
        return carry

    lax.fori_loop(0, nc, p2, 0, unroll=RET_UNROLL)


def _ret_tables(t, c_len, dh):
    C = RET_CHUNK
    hh = jnp.arange(RET_HEADS, dtype=F32)
    log_g = jnp.log1p(-jnp.exp2(-(5.0 + 2.0 * hh)))
    idx = jnp.arange(C, dtype=F32)
    dm = jnp.exp(jnp.abs(idx[:, None] - idx[None, :])[None] * log_g[:, None, None])

    def lanes(e):
        return jnp.broadcast_to(jnp.exp(e[None, :] * log_g[:, None])[:, :, None], (RET_HEADS, e.shape[0], dh))

    rt = jnp.stack([lanes(idx + 1.0), lanes(C - idx), lanes(C - 1.0 - idx), lanes(idx)], axis=1)
    pos = jnp.arange(c_len, dtype=F32)
    cwt = jnp.stack([lanes(c_len - 1.0 - pos), lanes(pos)], axis=1)
    gc = jnp.broadcast_to(jnp.exp(C * log_g)[:, None, None], (RET_HEADS, SUBLANES, dh))
    rows = t // GRID_W
    row = jnp.repeat(jnp.arange(rows, dtype=F32), GRID_W)
    col = jnp.tile(jnp.arange(GRID_W, dtype=F32), rows)
    q4 = dh // 4
    freqs = ROPE_BASE ** (-jnp.arange(q4, dtype=F32) / q4)
    ang_r = row[:, None] * freqs[None, :]
    ang_c = col[:, None] * freqs[None, :]
    cos = jnp.concatenate([jnp.cos(ang_r)] * 2 + [jnp.cos(ang_c)] * 2, axis=1)
    sin = jnp.concatenate([-jnp.sin(ang_r), jnp.sin(ang_r), -jnp.sin(ang_c), jnp.sin(ang_c)], axis=1)
    return dm, rt, cwt, gc, cos, sin


def _ret_call(qkvg, qkvg_c):
    b, t, w = qkvg.shape
    c_len = qkvg_c.shape[1]
    dh = w // (4 * RET_HEADS)
    nc = t // RET_CHUNK
    dm, rt, cwt, gc, cos, sin = _ret_tables(t, c_len, dh)
    H = RET_HEADS

    def col(off):
        return lambda bi, h: (bi, 0, off + h)

    return pl.pallas_call(
        functools.partial(_ret_kernel, nc=nc, scale=dh ** -0.5),
        grid=(b, H),
        in_specs=[
            pl.BlockSpec((1, t, dh), col(0)),
            pl.BlockSpec((1, t, dh), col(H)),
            pl.BlockSpec((1, t, dh), col(2 * H)),
            pl.BlockSpec((1, t, dh), col(3 * H)),
            pl.BlockSpec((1, c_len, dh), col(H)),
            pl.BlockSpec((1, c_len, dh), col(2 * H)),
            pl.BlockSpec((t, dh), lambda bi, h: (0, 0)),
            pl.BlockSpec((t, dh), lambda bi, h: (0, 0)),
            pl.BlockSpec((1, RET_CHUNK, RET_CHUNK), lambda bi, h: (h, 0, 0)),
            pl.BlockSpec((1, 4, RET_CHUNK, dh), lambda bi, h: (h, 0, 0, 0)),
            pl.BlockSpec((1, 2, c_len, dh), lambda bi, h: (h, 0, 0, 0)),
            pl.BlockSpec((1, SUBLANES, dh), lambda bi, h: (h, 0, 0)),
        ],
        out_specs=pl.BlockSpec((1, t, dh), col(0)),
        out_shape=jax.ShapeDtypeStruct((b, t, H * dh), BF16),
        scratch_shapes=[
            pltpu.VMEM((t, dh), F32),
            pltpu.VMEM((t, dh), BF16),
            pltpu.VMEM((nc, dh, dh), F32),
            pltpu.VMEM((nc, dh, dh), F32),
        ],
        compiler_params=_cparams("parallel", "arbitrary"),
        name="retention",
    )(qkvg, qkvg, qkvg, qkvg, qkvg_c, qkvg_c, cos, sin, dm, rt, cwt, gc)


def _route(logits_t, bias):
    e, n = logits_t.shape
    per = e // N_GROUPS
    s = jax.nn.sigmoid(logits_t)
    sb = s + bias
    neg = -jnp.inf
    g3 = sb.reshape(N_GROUPS, per, n)
    it3 = lax.broadcasted_iota(jnp.int32, g3.shape, 1)
    m1 = jnp.max(g3, axis=1, keepdims=True)
    i1 = jnp.min(jnp.where(g3 == m1, it3, per), axis=1, keepdims=True)
    m2 = jnp.max(jnp.where(it3 == i1, neg, g3), axis=1, keepdims=True)
    gs = (m1 + m2).reshape(N_GROUPS, n)
    itg = lax.broadcasted_iota(jnp.int32, gs.shape, 0)
    gsel = jnp.zeros(gs.shape, jnp.bool_)
    for _ in range(TOP_GROUPS):
        m = jnp.max(gs, axis=0, keepdims=True)
        i = jnp.min(jnp.where(gs == m, itg, N_GROUPS), axis=0, keepdims=True)
        hit = itg == i
        gsel = gsel | hit
        gs = jnp.where(hit, neg, gs)
    gmask = jnp.broadcast_to(gsel.reshape(N_GROUPS, 1, n), (N_GROUPS, per, n)).reshape(e, n)
    v = jnp.where(gmask, sb, neg)
    ite = lax.broadcasted_iota(jnp.int32, v.shape, 0)
    sel = jnp.zeros(v.shape, jnp.bool_)
    for _ in range(TOP_K):
        m = jnp.max(v, axis=0, keepdims=True)
        i = jnp.min(jnp.where(v == m, ite, e), axis=0, keepdims=True)
        hit = ite == i
        sel = sel | hit
        v = jnp.where(hit, neg, v)
    picked = jnp.where(sel, s, 0.0)
    return picked / jnp.sum(picked, axis=0, keepdims=True) * ROUTE_SCALE


def _outproj_kernel(yl_ref, yr_ref, x_ref, g1_ref, sh_ref, sc_ref, nf_ref, wo_ref, wrt_ref, rb_ref,
                    x1_ref, h2_ref, gt_ref, cnt_ref):
    dl = yl_ref.shape[2]
    ts = gt_ref.shape[1] // cnt_ref.shape[0]
    for k in range(cnt_ref.shape[0]):
        rows = pl.ds(k * ts, ts)
        mx = jnp.dot(yl_ref[0, rows, :], wo_ref[0:dl, :], preferred_element_type=F32)
        mx = mx + jnp.dot(yr_ref[0, rows, :], wo_ref[dl:, :], preferred_element_type=F32)
        x1 = x_ref[0, rows, :] + g1_ref[0] * mx
        x1_ref[0, rows, :] = x1
        h = _rms_mod(x1, nf_ref[...], sh_ref[0], sc_ref[0])
        h2_ref[0, rows, :] = h.astype(h2_ref.dtype)
        logits_t = lax.dot_general(wrt_ref[...], h, NT_DIMS, precision=HIGHEST, preferred_element_type=F32)
        gates = _route(logits_t, rb_ref[...])
        gt_ref[:, rows] = gates
        hit = (gates > 0.0).astype(BF16)
        cnt_ref[k] = lax.dot_general(jnp.ones((SUBLANES, ts), BF16), hit, NT_DIMS, preferred_element_type=F32)


def _outproj_call(y_lru, y_ret, x, g1, sh2, sc2, norm_ffn, w_out_bf, w_router_t, b_router):
    b, t, d = x.shape
    dl = y_lru.shape[2]
    e = w_router_t.shape[0]
    ts = min(t, MOE_TILE)
    tt = min(t, OUTPROJ_TILES * ts)
    nt = t // tt
    per = tt // ts
    mod_map = lambda bi, i: (bi, 0, 0)
    return pl.pallas_call(
        _outproj_kernel,
        grid=(b, nt),
        in_specs=[
            pl.BlockSpec((1, tt, dl), lambda bi, i: (bi, i, 0)),
            pl.BlockSpec((1, tt, d - dl), lambda bi, i: (bi, i, 0)),
            pl.BlockSpec((1, tt, d), lambda bi, i: (bi, i, 0)),
            pl.BlockSpec((1, 1, d), mod_map),
            pl.BlockSpec((1, 1, d), mod_map),
            pl.BlockSpec((1, 1, d), mod_map),
            pl.BlockSpec((1, d), lambda bi, i: (0, 0)),
            pl.BlockSpec((d, d), lambda bi, i: (0, 0)),
            pl.BlockSpec((e, d), lambda bi, i: (0, 0)),
            pl.BlockSpec((e, 1), lambda bi, i: (0, 0)),
        ],
        out_specs=[
            pl.BlockSpec((1, tt, d), lambda bi, i: (bi, i, 0)),
            pl.BlockSpec((1, tt, d), lambda bi, i: (bi, i, 0)),
            pl.BlockSpec((e, tt), lambda bi, i: (0, bi * nt + i)),
            pl.BlockSpec((per, SUBLANES, e), lambda bi, i: (bi * nt + i, 0, 0)),
        ],
        out_shape=[
            jax.ShapeDtypeStruct((b, t, d), F32),
            jax.ShapeDtypeStruct((b, t, d), BF16),
            jax.ShapeDtypeStruct((e, b * t), F32),
            jax.ShapeDtypeStruct((b * nt * per, SUBLANES, e), F32),
        ],
        compiler_params=_cparams("parallel", "arbitrary"),
        name="outproj_route",
    )(y_lru, y_ret, x, g1, sh2, sc2, norm_ffn.reshape(1, d), w_out_bf, w_router_t, b_router.reshape(e, 1))


MOE_TILE = 256
MOE_CHUNK = 16
MOE_RB = 512
MOE_BM = 512
OUTPROJ_TILES = 2
NO_RANK = 512.0

TN_DIMS = (((0,), (0,)), ((), ()))
TAB_LOFF, TAB_GBASE, TAB_CPAD, TAB_NROWS, TAB_EEND = range(5)


def _round_up(v, m):
    return -(-v // m) * m


def _tables_kernel(cnt_ref, loff_ref, gbase_ref, cpad_ref, eend_ref, blk_ref, nused_ref, ctab_ref, *, nb):
    cnt = cnt_ref[...]
    ns, ne = cnt.shape
    cpad = jnp.ceil(cnt * (1.0 / MOE_CHUNK)) * MOE_CHUNK
    upper = (lax.broadcasted_iota(jnp.int32, (ne, ne), 0) < lax.broadcasted_iota(jnp.int32, (ne, ne), 1)).astype(F32)
    lower = (lax.broadcasted_iota(jnp.int32, (ns, ns), 1) < lax.broadcasted_iota(jnp.int32, (ns, ns), 0)).astype(F32)
    loff = jnp.dot(cpad, upper, precision=HIGHEST, preferred_element_type=F32)
    before = jnp.dot(lower, cpad, precision=HIGHEST, preferred_element_type=F32)
    tot = jnp.sum(cpad, axis=0, keepdims=True)
    epad = jnp.ceil(tot * (1.0 / MOE_BM)) * MOE_BM
    epad8 = jnp.broadcast_to(epad, (SUBLANES, ne))
    ebase = jnp.dot(epad8, upper, precision=HIGHEST, preferred_element_type=F32)[0:1]
    eend = ebase + epad
    loff_ref[...] = loff
    gbase_ref[...] = ebase + before
    cpad_ref[...] = cpad
    eend_ref[...] = eend
    start = lax.broadcasted_iota(jnp.int32, (nb, ne), 0).astype(F32) * MOE_BM
    blk = jnp.sum((eend <= start).astype(F32), axis=1, keepdims=True)
    blk_ref[...] = jnp.minimum(blk, ne - 1.0)
    nused_ref[...] = jnp.sum(epad, axis=1, keepdims=True) * (1.0 / MOE_BM)
    nch = ctab_ref.shape[1]
    cidx = lax.broadcasted_iota(jnp.int32, (ns, nch), 1).astype(F32)
    shift = (ebase + before - loff) * (1.0 / MOE_CHUNK)
    ctab = jnp.full((ns, nch), -1.0, F32)
    for e in range(ne):
        lo = loff[:, e:e + 1] * (1.0 / MOE_CHUNK)
        hi = lo + cpad[:, e:e + 1] * (1.0 / MOE_CHUNK)
        ctab = jnp.where((cidx >= lo) & (cidx < hi), cidx + shift[:, e:e + 1], ctab)
    ctab_ref[...] = ctab


def _tables_call(cnt, nb, nch):
    ns, ne = cnt.shape
    f = lambda shape: jax.ShapeDtypeStruct(shape, F32)
    return pl.pallas_call(
        functools.partial(_tables_kernel, nb=nb),
        out_shape=[f((ns, ne)), f((ns, ne)), f((ns, ne)), f((1, ne)), f((nb, 1)), f((1, 1)), f((ns, nch))],
        compiler_params=pltpu.CompilerParams(vmem_limit_bytes=VMEM_LIMIT_BYTES),
        name="moe_tables",
    )(cnt)


def _tile_rank_rhs(gt, lcol):
    ne, ts = gt.shape
    hit = gt > 0.0
    before = (lax.broadcasted_iota(jnp.int32, (ts, ts), 0) < lax.broadcasted_iota(jnp.int32, (ts, ts), 1)).astype(BF16)
    rank = jnp.dot(hit.astype(BF16), before, preferred_element_type=F32)
    rm = jnp.where(hit, rank, NO_RANK)
    q = jnp.broadcast_to(lcol * (1.0 / MOE_CHUNK), (ne, LANES))
    return jnp.concatenate([rm, q], axis=1).astype(BF16)


def _onehot_rows(r0, rhs, lrow, crow, ts):
    ne = lrow.shape[1]
    rows = (r0 + lax.broadcasted_iota(jnp.int32, (MOE_RB, ne), 0)).astype(F32)
    seg = ((rows >= lrow) & (rows < crow)).astype(BF16)
    ex = jnp.dot(seg, rhs, preferred_element_type=F32)
    rowl = (r0 + lax.broadcasted_iota(jnp.int32, (MOE_RB, LANES), 0)).astype(F32)
    tgt = rowl - MOE_CHUNK * ex[:, ts:ts + LANES]
    p = ex[:, :ts] == jnp.concatenate([tgt] * (ts // LANES), axis=1)
    return p, (ex[:, ts + LANES:] if rhs.shape[1] > ts + LANES else None)


def _chunk_copy(buf, slot, c, row, hbm, sem, to_hbm):
    loc = buf.at[slot, pl.ds(c * MOE_CHUNK, MOE_CHUNK)]
    glob = hbm.at[pl.ds(pl.multiple_of(row, MOE_CHUNK), MOE_CHUNK)]
    return pltpu.make_async_copy(loc, glob, sem.at[slot]) if to_hbm else pltpu.make_async_copy(glob, loc, sem.at[slot])


def _wait_chunks(n, buf, slot, hbm, sem, to_hbm):
    for _ in range(n):
        _chunk_copy(buf, slot, 0, 0, hbm, sem, to_hbm).wait()


def _wait_chunks_dyn(n, buf, slot, hbm, sem, to_hbm):
    def body(i, c):
        _chunk_copy(buf, slot, 0, 0, hbm, sem, to_hbm).wait()
        return c

    lax.fori_loop(0, n, body, 0)


def _dispatch_kernel(tab_ref, ctab_ref, h_ref, gt_ref, lrow_ref, crow_ref, lcol_ref, xs_hbm, xbuf, zbuf, sem,
                     *, ns, trash0):
    s = pl.program_id(0)
    slot = s % 2
    ts = h_ref.shape[0]
    ne = gt_ref.shape[0]
    rows_max = xbuf.shape[1]
    nch = rows_max // MOE_CHUNK

    @pl.when(s >= 2)
    def _():
        _wait_chunks(nch, xbuf, slot, xs_hbm, sem, True)

    rhs = _tile_rank_rhs(gt_ref[...], lcol_ref[0])
    for r0 in range(0, rows_max, MOE_RB):
        p, _ = _onehot_rows(r0, rhs, lrow_ref[0], crow_ref[0], ts)
        xbuf[slot, r0:r0 + MOE_RB, :] = jnp.dot(p.astype(BF16), h_ref[...],
                                                preferred_element_type=F32).astype(xbuf.dtype)
    for c in range(nch):
        g = ctab_ref[0, 0, c]
        row = jnp.where(g >= 0, g * MOE_CHUNK, trash0 + slot * rows_max + c * MOE_CHUNK)
        _chunk_copy(xbuf, slot, c, row, xs_hbm, sem, True).start()

    @pl.when(s == ns - 1)
    def _():
        zbuf[...] = jnp.zeros(zbuf.shape, zbuf.dtype)

        def per_expert(e, total):
            first = tab_ref[0, TAB_GBASE, e] + tab_ref[0, TAB_CPAD, e]
            ntail = lax.shift_right_logical(tab_ref[0, TAB_EEND, e] - first, MOE_CHUNK.bit_length() - 1)

            def per_chunk(j, c2):
                dst = xs_hbm.at[pl.ds(pl.multiple_of(first + j * MOE_CHUNK, MOE_CHUNK), MOE_CHUNK)]
                pltpu.make_async_copy(zbuf, dst, sem.at[slot]).start()
                return c2

            lax.fori_loop(0, ntail, per_chunk, 0)
            return total + ntail

        ntail = lax.fori_loop(0, ne, per_expert, 0)
        _wait_chunks_dyn(ntail, xbuf, slot, xs_hbm, sem, True)
        _wait_chunks(nch, xbuf, slot, xs_hbm, sem, True)
        if ns >= 2:
            _wait_chunks(nch, xbuf, 1 - slot, xs_hbm, sem, True)


def _dispatch_call(tab, ctab, h2, gt, lrow, crow, lcol, p_max, rows_max):
    n, d = h2.shape
    ne = gt.shape[0]
    ts = min(n, MOE_TILE)
    ns = n // ts
    nch = rows_max // MOE_CHUNK
    return pl.pallas_call(
        functools.partial(_dispatch_kernel, ns=ns, trash0=p_max),
        grid=(ns,),
        in_specs=[
            pl.BlockSpec((1, 5, ne), lambda s: (s, 0, 0), memory_space=pltpu.SMEM),
            pl.BlockSpec((1, 1, nch), lambda s: (s, 0, 0), memory_space=pltpu.SMEM),
            pl.BlockSpec((ts, d), lambda s: (s, 0)),
            pl.BlockSpec((ne, ts), lambda s: (0, s)),
            pl.BlockSpec((1, 1, ne), lambda s: (s, 0, 0)),
            pl.BlockSpec((1, 1, ne), lambda s: (s, 0, 0)),
            pl.BlockSpec((1, ne, 1), lambda s: (s, 0, 0)),
        ],
        out_specs=pl.BlockSpec(memory_space=pl.ANY),
        out_shape=jax.ShapeDtypeStruct((p_max + 2 * rows_max, d), BF16),
        scratch_shapes=[
            pltpu.VMEM((2, rows_max, d), BF16),
            pltpu.VMEM((MOE_CHUNK, d), BF16),
            pltpu.SemaphoreType.DMA((2,)),
        ],
        compiler_params=_cparams("arbitrary"),
        name="moe_dispatch",
    )(tab, ctab, h2, gt, lrow, crow, lcol)


def _experts_kernel(blk_ref, nused_ref, x_ref, w1_ref, w3_ref, w2_ref, y_ref):
    del blk_ref

    @pl.when(pl.program_id(0) < nused_ref[0])
    def _():
        xv = x_ref[...]
        hid = _silu(jnp.dot(xv, w1_ref[0], preferred_element_type=F32)) * jnp.dot(xv, w3_ref[0], preferred_element_type=F32)
        y_ref[...] = jnp.dot(hid.astype(BF16), w2_ref[0], preferred_element_type=F32).astype(y_ref.dtype)


def _experts_call(blk, nused, xs, w1, w3, w2):
    d = xs.shape[1]
    _, _, de = w1.shape
    nb = blk.shape[0]
    p_max = nb * MOE_BM
    row_map = lambda i, blk_r, nu_r: (jnp.minimum(i, nu_r[0] - 1), 0)
    w_map = lambda i, blk_r, nu_r: (blk_r[i], 0, 0)
    return pl.pallas_call(
        _experts_kernel,
        grid_spec=pltpu.PrefetchScalarGridSpec(
            num_scalar_prefetch=2,
            grid=(nb,),
            in_specs=[
                pl.BlockSpec((MOE_BM, d), row_map),
                pl.BlockSpec((1, d, de), w_map),
                pl.BlockSpec((1, d, de), w_map),
                pl.BlockSpec((1, de, d), w_map),
            ],
            out_specs=pl.BlockSpec((MOE_BM, d), row_map),
        ),
        out_shape=jax.ShapeDtypeStruct((p_max, d), BF16),
        compiler_params=_cparams("arbitrary"),
        name="moe_experts",
    )(blk, nused, xs, w1, w3, w2)


def _combine_kernel(ctab_ref, ctabn_ref, gt_ref, lrow_ref, crow_ref, lcol_ref, h_ref, x1_ref, g2_ref,
                    s1_ref, s3_ref, s2_ref, nfin_ref, ys_hbm, o_ref, ybuf, sem, *, ns):
    s = pl.program_id(0)
    slot = s % 2
    ts = h_ref.shape[0]
    rows_max = ybuf.shape[1]
    nch = rows_max // MOE_CHUNK

    def gather(tab, sl):
        for c in range(nch):
            row = jnp.maximum(tab[0, 0, c], 0) * MOE_CHUNK
            _chunk_copy(ybuf, sl, c, row, ys_hbm, sem, False).start()

    @pl.when(s == 0)
    def _():
        gather(ctab_ref, 0)

    gather(ctabn_ref, 1 - slot)

    h = h_ref[...]
    hs = _silu(jnp.dot(h, s1_ref[...], preferred_element_type=F32)) * jnp.dot(h, s3_ref[...], preferred_element_type=F32)
    acc = jnp.dot(hs.astype(BF16), s2_ref[...], preferred_element_type=F32)

    gt = gt_ref[...]
    rhs = jnp.concatenate([_tile_rank_rhs(gt, lcol_ref[0]), gt.astype(BF16)], axis=1)
    _wait_chunks(nch, ybuf, slot, ys_hbm, sem, False)
    for r0 in range(0, rows_max, MOE_RB):
        p, gexp = _onehot_rows(r0, rhs, lrow_ref[0], crow_ref[0], ts)
        gm = jnp.where(p, gexp, 0.0).astype(BF16)
        acc = acc + lax.dot_general(gm, ybuf[slot, r0:r0 + MOE_RB, :], TN_DIMS, preferred_element_type=F32)
    xo = x1_ref[...] + g2_ref[0] * acc
    o_ref[...] = xo * lax.rsqrt(jnp.mean(xo * xo, axis=-1, keepdims=True) + EPS) * nfin_ref[...]

    @pl.when(s == ns - 1)
    def _():
        _wait_chunks(nch, ybuf, 1 - slot, ys_hbm, sem, False)


def _combine_call(ctab, gt, lrow, crow, lcol, h2, x1, g2, ws1, ws3, ws2, norm_final, ys, rows_max, t):
    n, d = h2.shape
    ne = gt.shape[0]
    ts = min(n, MOE_TILE)
    ns = n // ts
    per_b = t // ts
    nch = rows_max // MOE_CHUNK
    return pl.pallas_call(
        functools.partial(_combine_kernel, ns=ns),
        grid=(ns,),
        in_specs=[
            pl.BlockSpec((1, 1, nch), lambda s: (s, 0, 0), memory_space=pltpu.SMEM),
            pl.BlockSpec((1, 1, nch), lambda s: (jnp.minimum(s + 1, ns - 1), 0, 0), memory_space=pltpu.SMEM),
            pl.BlockSpec((ne, ts), lambda s: (0, s)),
            pl.BlockSpec((1, 1, ne), lambda s: (s, 0, 0)),
            pl.BlockSpec((1, 1, ne), lambda s: (s, 0, 0)),
            pl.BlockSpec((1, ne, 1), lambda s: (s, 0, 0)),
            pl.BlockSpec((ts, d), lambda s: (s, 0)),
            pl.BlockSpec((ts, d), lambda s: (s, 0)),
            pl.BlockSpec((1, 1, d), lambda s: (s // per_b, 0, 0)),
            pl.BlockSpec(ws1.shape, lambda s: (0, 0)),
            pl.BlockSpec(ws3.shape, lambda s: (0, 0)),
            pl.BlockSpec(ws2.shape, lambda s: (0, 0)),
            pl.BlockSpec((1, d), lambda s: (0, 0)),
            pl.BlockSpec(memory_space=pl.ANY),
        ],
        out_specs=pl.BlockSpec((ts, d), lambda s: (s, 0)),
        out_shape=jax.ShapeDtypeStruct((n, d), F32),
        scratch_shapes=[
            pltpu.VMEM((2, rows_max, d), BF16),
            pltpu.SemaphoreType.DMA((2,)),
        ],
        compiler_params=_cparams("arbitrary"),
        name="moe_combine",
    )(ctab, ctab, gt, lrow, crow, lcol, h2, x1, g2, ws1, ws3, ws2, norm_final.reshape(1, d), ys)


def _moe_call(h2, gt, cnt, x1, g2, w1, w3, w2, ws1, ws3, ws2, norm_final, t):
    n, d = h2.shape
    ne = gt.shape[0]
    ts = min(n, MOE_TILE)
    ns = n // ts
    rows_max = _round_up(TOP_K * ts + ne * (MOE_CHUNK - 1), MOE_RB)
    p_max = _round_up(n * TOP_K + ns * ne * (MOE_CHUNK - 1) + ne * (MOE_BM - 1), MOE_BM)
    nb = p_max // MOE_BM

    nch = rows_max // MOE_CHUNK
    loff, gbase, cpad, eend, blk, nused, ctab = _tables_call(cnt[:, 0, :], nb, nch)
    nrows = loff[:, ne - 1:] + cpad[:, ne - 1:]
    tab = jnp.stack([loff, gbase, cpad, jnp.broadcast_to(nrows, (ns, ne)), jnp.broadcast_to(eend, (ns, ne))],
                    axis=1).astype(jnp.int32)
    ctab = ctab.astype(jnp.int32).reshape(ns, 1, nch)
    lrow = loff.reshape(ns, 1, ne)
    crow = (loff + cpad).reshape(ns, 1, ne)
    lcol = loff.reshape(ns, ne, 1)

    xs = _dispatch_call(tab, ctab, h2, gt, lrow, crow, lcol, p_max, rows_max)
    ys = _experts_call(blk.reshape(nb).astype(jnp.int32), nused.reshape(1).astype(jnp.int32), xs, w1, w3, w2)
    return _combine_call(ctab, gt, lrow, crow, lcol, h2, x1, g2, ws1, ws3, ws2, norm_final, ys, rows_max, t)


def _block_diag(w):
    nb, n, _ = w.shape
    eye = jnp.eye(nb, dtype=w.dtype)
    return (eye[:, None, :, None] * w[:, :, None, :]).reshape(nb * n, nb * n)


def _lru_gate_weights(w_rec, b_rec, w_inp, b_inp):
    d_lru = b_rec.shape[1]
    half = d_lru // 2
    ws, bs = [], []
    for dirn in range(2):
        wr, wi = _block_diag(w_rec[dirn]), _block_diag(w_inp[dirn])
        ws.append(jnp.stack([jnp.concatenate([wr[s:s + half, s:s + half], wi[s:s + half, s:s + half]], axis=1)
                             for s in (0, half)]))
        bs.append(jnp.stack([jnp.concatenate([b_rec[dirn, s:s + half], b_inp[dirn, s:s + half]])[None, :]
                             for s in (0, half)]))
    return jnp.stack(ws).astype(BF16), jnp.stack(bs).astype(F32)


def kernel(x, c, ctx, c_ctx, w_mod, b_mod, norm_mix, norm_ffn, w_in, conv_w, conv_b, lru_w_rec, lru_b_rec, lru_w_in, lru_b_in, lru_lambda, w_out, w_router, b_router, w_exp_gate, w_exp_up, w_exp_down, w_sh_gate, w_sh_up, w_sh_down, norm_final):
    b, t, d = x.shape
    depth = w_mod.shape[0]
    assert depth == 1, "context-stream update for deeper stacks is not implemented"
    l = 0
    d_lru = conv_w.shape[2]
    n_xg = 2 * d_lru

    rows = -(-(b + 1) // SUBLANES) * SUBLANES
    cc = jnp.concatenate([c, c_ctx[None, :], jnp.zeros((rows - b - 1, d), F32)], axis=0)
    mod = _mod_call(cc, w_mod[l], b_mod[l]).reshape(rows, 1, -1)
    sh1, sc1, g1, sh2, sc2, g2 = (mod[:, :, i * d:(i + 1) * d] for i in range(6))

    w_in_bf = w_in[l].astype(BF16)
    gain_mix = norm_mix[l].reshape(1, d)
    xg, qkvg = _inproj_call(x, sh1[:b], sc1[:b], gain_mix, w_in_bf, n_xg, True)
    xg_c, qkvg_c = _inproj_call(ctx, sh1[b:b + 1], sc1[b:b + 1], gain_mix, w_in_bf, n_xg, False)

    wg, bg = _lru_gate_weights(lru_w_rec[l], lru_b_rec[l], lru_w_in[l], lru_b_in[l])
    y_lru = _lru_call(xg, xg_c, conv_w[l], conv_b[l], wg, bg, lru_lambda[l])
    y_ret = _ret_call(qkvg, qkvg_c)

    x1, h2, gt, cnt = _outproj_call(y_lru, y_ret, x, g1[:b], sh2[:b], sc2[:b], norm_ffn[l], w_out[l].astype(BF16),
                                    w_router[l].T, b_router[l])

    out = _moe_call(h2.reshape(b * t, d), gt, cnt, x1.reshape(b * t, d), g2[:b],
                    w_exp_gate[l].astype(BF16), w_exp_up[l].astype(BF16), w_exp_down[l].astype(BF16),
                    w_sh_gate[l].astype(BF16), w_sh_up[l].astype(BF16), w_sh_down[l].astype(BF16), norm_final, t)
    return out.reshape(b, t, d)
```

```python
import functools

import jax
import jax.numpy as jnp
from jax import lax
from jax.experimental import pallas as pl
from jax.experimental.pallas import tpu as pltpu

F32 = jnp.float32
BF16 = jnp.bfloat16
HIGHEST = lax.Precision.HIGHEST

EPS = 1e-6
GRID_W = 64
LRU_C = 8.0
RET_HEADS = 4
RET_CHUNK = 128
RET_UNROLL = 8
ROPE_BASE = 10000.0
N_GROUPS = 8
TOP_GROUPS = 4
TOP_K = 6
ROUTE_SCALE = 2.5

LANES = 128
SUBLANES = 8
VMEM_LIMIT_BYTES = 56 * 1024 * 1024

NT_DIMS = (((1,), (1,)), ((), ()))


def _cparams(*sem):
    return pltpu.CompilerParams(dimension_semantics=sem, vmem_limit_bytes=VMEM_LIMIT_BYTES)


def _silu(v):
    return v * jax.nn.sigmoid(v)


def _dot32(a, b):
    return jnp.dot(a, b, preferred_element_type=F32)


def _rms_mod(x, gain, shift, scale):
    y = x * lax.rsqrt(jnp.mean(x * x, axis=-1, keepdims=True) + EPS) * gain
    return y * (1.0 + scale) + shift


def _mod_kernel(c_ref, w_ref, b_ref, o_ref):
    s = _silu(c_ref[...])
    o_ref[...] = jnp.dot(s, w_ref[...], precision=HIGHEST, preferred_element_type=F32) + b_ref[...]


def _mod_call(cc, w_mod, b_mod):
    rows, d = cc.shape
    n = w_mod.shape[1]
    tn = 512
    return pl.pallas_call(
        _mod_kernel,
        grid=(n // tn,),
        in_specs=[
            pl.BlockSpec((rows, d), lambda j: (0, 0)),
            pl.BlockSpec((d, tn), lambda j: (0, j)),
            pl.BlockSpec((1, tn), lambda j: (0, j)),
        ],
        out_specs=pl.BlockSpec((rows, tn), lambda j: (0, j)),
        out_shape=jax.ShapeDtypeStruct((rows, n), F32),
        compiler_params=_cparams("arbitrary"),
        name="mod",
    )(cc, w_mod, b_mod.reshape(1, n))


def _inproj_kernel(x_ref, sh_ref, sc_ref, g_ref, w_ref, xg_ref, qkvg_ref, *, n_xg):
    h = _rms_mod(x_ref[0], g_ref[...], sh_ref[0], sc_ref[0]).astype(BF16)
    xg_ref[0] = jnp.dot(h, w_ref[:, :n_xg], preferred_element_type=F32)
    qkvg_ref[0] = jnp.dot(h, w_ref[:, n_xg:], preferred_element_type=F32)


def _inproj_call(x, shift, scale, gain, w_in_bf, n_xg, per_batch_mod):
    b, t, d = x.shape
    n_all = w_in_bf.shape[1]
    tt = min(t, 512)
    mod_map = (lambda bi, i: (bi, 0, 0)) if per_batch_mod else (lambda bi, i: (0, 0, 0))
    return pl.pallas_call(
        functools.partial(_inproj_kernel, n_xg=n_xg),
        grid=(b, t // tt),
        in_specs=[
            pl.BlockSpec((1, tt, d), lambda bi, i: (bi, i, 0)),
            pl.BlockSpec((1, 1, d), mod_map),
            pl.BlockSpec((1, 1, d), mod_map),
            pl.BlockSpec((1, d), lambda bi, i: (0, 0)),
            pl.BlockSpec((d, n_all), lambda bi, i: (0, 0)),
        ],
        out_specs=[
            pl.BlockSpec((1, tt, n_xg), lambda bi, i: (bi, i, 0)),
            pl.BlockSpec((1, tt, n_all - n_xg), lambda bi, i: (bi, i, 0)),
        ],
        out_shape=[
            jax.ShapeDtypeStruct((b, t, n_xg), F32),
            jax.ShapeDtypeStruct((b, t, n_all - n_xg), F32),
        ],
        compiler_params=_cparams("parallel", "arbitrary"),
        name="inproj",
    )(x, shift, scale, gain, w_in_bf)


SCAN_ROWS = 64


def _tile_scan(a, b, reverse):
    row = lax.broadcasted_iota(jnp.int32, a.shape, 0)
    for s in (1, 2, 4):
        if reverse:
            a_s = pltpu.roll(a, SUBLANES - s, 0)
            b_s = pltpu.roll(b, SUBLANES - s, 0)
            m = row < SUBLANES - s
        else:
            a_s = pltpu.roll(a, s, 0)
            b_s = pltpu.roll(b, s, 0)
            m = row >= s
        b = jnp.where(m, a * b_s + b, b)
        a = jnp.where(m, a * a_s, a)
    return a, b


def _lru_kernel(xa_ref, ga_ref, xac_ref, cw_ref, cb_ref, wg_ref, bg_ref, lam_ref, y_ref,
                hf_ref, xe_ref, a_ref, b_ref, carry_ref, *, t_len, c_len, tc, nc):
    j = pl.program_id(1)
    half = wg_ref.shape[2]

    def gates(n, dirn):
        u = cb_ref[...]
        for k in range(cw_ref.shape[0]):
            u = u + cw_ref[k:k + 1, :] * xe_ref[6 + k:6 + k + n, :]
        lam = -lam_ref[dirn:dirn + 1, :]
        sp = jnp.maximum(lam, 0.0) + jnp.log1p(jnp.exp(-jnp.abs(lam)))
        for hf in range(2):
            uh = u[:, hf * half:(hf + 1) * half]
            z = jnp.dot(uh.astype(BF16), wg_ref[dirn, hf], preferred_element_type=F32) + bg_ref[dirn, hf]
            r = jax.nn.sigmoid(z[:, :half])
            i = jax.nn.sigmoid(z[:, half:])
            log_a = -LRU_C * r * sp[:, hf * half:(hf + 1) * half]
            a = jnp.exp(log_a)
            a_ref[0:n, hf * half:(hf + 1) * half] = a
            b_ref[0:n, hf * half:(hf + 1) * half] = jnp.sqrt((1.0 - a) * (1.0 + a)) * (i * uh)

    def scan(n, dirn, emit):
        ng = n // SCAN_ROWS
        nt = SCAN_ROWS // SUBLANES

        def body(gi, carry):
            g = gi if dirn == 0 else ng - 1 - gi
            base = pl.multiple_of(g * SCAN_ROWS, SCAN_ROWS)
            av = a_ref[pl.ds(base, SCAN_ROWS), :]
            bv = b_ref[pl.ds(base, SCAN_ROWS), :]
            hs = [None] * nt
            order = range(nt) if dirn == 0 else range(nt - 1, -1, -1)
            for k in order:
                a, b = _tile_scan(av[k * SUBLANES:(k + 1) * SUBLANES], bv[k * SUBLANES:(k + 1) * SUBLANES],
                                  dirn == 1)
                h = a * carry + b
                carry = h[SUBLANES - 1:SUBLANES] if dirn == 0 else h[0:1]
                hs[k] = h
            if emit is not None:
                emit(base, jnp.concatenate(hs, axis=0))
            return carry

        carry_ref[...] = lax.fori_loop(0, ng, body, carry_ref[...])

    def run(dirn):
        first = (j == 0) if dirn == 0 else (j == nc)
        ci = j if dirn == 0 else 2 * nc - 1 - j

        @pl.when(first)
        def _():
            zeros = jnp.zeros((SUBLANES, xe_ref.shape[1]), F32)
            xe_ref[0:SUBLANES] = zeros
            xe_ref[SUBLANES:SUBLANES + c_len] = xac_ref[0]
            xe_ref[SUBLANES + c_len:2 * SUBLANES + c_len] = zeros
            gates(c_len, dirn)
            carry_ref[...] = jnp.zeros(carry_ref.shape, F32)
            scan(c_len, dirn, None)

        r0 = pl.multiple_of(ci * tc, tc)
        p0 = pl.multiple_of(jnp.maximum(r0 - SUBLANES, 0), SUBLANES)
        n0 = pl.multiple_of(jnp.minimum(r0 + tc, t_len - SUBLANES), SUBLANES)
        prev = xa_ref[0, pl.ds(p0, SUBLANES), :]
        nxt = xa_ref[0, pl.ds(n0, SUBLANES), :]
        xe_ref[0:SUBLANES] = jnp.where(ci > 0, prev, 0.0)
        xe_ref[SUBLANES:SUBLANES + tc] = xa_ref[0, pl.ds(r0, tc), :]
        xe_ref[SUBLANES + tc:2 * SUBLANES + tc] = jnp.where(ci < nc - 1, nxt, 0.0)
        gates(tc, dirn)

        if dirn == 0:
            def emit(base, h):
                hf_ref[pl.ds(r0 + base, SCAN_ROWS), :] = h
        else:
            def emit(base, h):
                tot = hf_ref[pl.ds(r0 + base, SCAN_ROWS), :] + h
                ga = ga_ref[0, pl.ds(base, SCAN_ROWS), :]
                y_ref[0, pl.ds(base, SCAN_ROWS), :] = (tot * jax.nn.gelu(ga)).astype(y_ref.dtype)
        scan(tc, dirn, emit)

    @pl.when(j < nc)
    def _():
        run(0)

    @pl.when(j >= nc)
    def _():
        run(1)


def _lru_call(xg, xg_c, conv_w, conv_b, wg, bg, lam):
    b, t, _ = xg.shape
    c_len = xg_c.shape[1]
    d_lru = conv_w.shape[1]
    tc = min(t, 256)
    nc = t // tc
    assert c_len <= tc and c_len % SCAN_ROWS == 0 and tc % SCAN_ROWS == 0

    def chunk_map(bi, j):
        return (bi, jnp.where(j < nc, nc - 1, 2 * nc - 1 - j), 0)

    def ga_map(bi, j):
        return (bi, jnp.where(j < nc, nc - 1, 2 * nc - 1 - j), 1)

    return pl.pallas_call(
        functools.partial(_lru_kernel, t_len=t, c_len=c_len, tc=tc, nc=nc),
        grid=(b, 2 * nc),
        in_specs=[
            pl.BlockSpec((1, t, d_lru), lambda bi, j: (bi, 0, 0)),
            pl.BlockSpec((1, tc, d_lru), ga_map),
            pl.BlockSpec((1, c_len, d_lru), lambda bi, j: (bi, 0, 0)),
            pl.BlockSpec(conv_w.shape, lambda bi, j: (0, 0)),
            pl.BlockSpec((1, d_lru), lambda bi, j: (0, 0)),
            pl.BlockSpec(wg.shape, lambda bi, j: (0, 0, 0, 0)),
            pl.BlockSpec(bg.shape, lambda bi, j: (0, 0, 0, 0)),
            pl.BlockSpec(lam.shape, lambda bi, j: (0, 0)),
        ],
        out_specs=pl.BlockSpec((1, tc, d_lru), chunk_map),
        out_shape=jax.ShapeDtypeStruct((b, t, d_lru), BF16),
        scratch_shapes=[
            pltpu.VMEM((t, d_lru), F32),
            pltpu.VMEM((tc + 2 * SUBLANES, d_lru), F32),
            pltpu.VMEM((tc, d_lru), F32),
            pltpu.VMEM((tc, d_lru), F32),
            pltpu.VMEM((1, d_lru), F32),
        ],
        compiler_params=_cparams("parallel", "arbitrary"),
        name="lru",
    )(xg, xg, xg_c, conv_w, conv_b.reshape(1, d_lru), wg, bg, lam)


def _ret_kernel(q_ref, k_ref, v_ref, g_ref, kc_ref, vc_ref, cos_ref, sin_ref, dm_ref, rt_ref, cwt_ref, gc_ref,
                y_ref, qr_ref, kr_ref, sf_ref, sb_ref, *, nc, scale):
    C = RET_CHUNK
    dh = q_ref.shape[2]
    lane = lax.broadcasted_iota(jnp.int32, (C, dh), 1)
    low = (lane % (dh // 2)) < (dh // 4)

    def rope(xv, cs, sn):
        partner = jnp.where(low, pltpu.roll(xv, dh - dh // 4, 1), pltpu.roll(xv, dh // 4, 1))
        return xv * cs + partner * sn

    xi_f, xi_b, ze_f, ze_b = rt_ref[0, 0], rt_ref[0, 1], rt_ref[0, 2], rt_ref[0, 3]
    g_chunk = gc_ref[0]
    g_full = jnp.concatenate([g_chunk] * (dh // SUBLANES), axis=0)

    kc = (kc_ref[0] * scale).T.astype(BF16)
    vc = vc_ref[0]
    s_f0 = jnp.dot(kc, (vc * cwt_ref[0, 0]).astype(BF16), preferred_element_type=F32)
    s_b0 = jnp.dot(kc, (vc * cwt_ref[0, 1]).astype(BF16), preferred_element_type=F32)

    def p1(n, s):
        r = pl.multiple_of(n * C, C)
        cs, sn = cos_ref[pl.ds(r, C), :], sin_ref[pl.ds(r, C), :]
        k = rope(k_ref[0, pl.ds(r, C), :], cs, sn) * scale
        qr_ref[pl.ds(r, C), :] = rope(q_ref[0, pl.ds(r, C), :], cs, sn)
        kr_ref[pl.ds(r, C), :] = k.astype(BF16)
        v = v_ref[0, pl.ds(r, C), :]
        vz = jnp.concatenate([v * ze_f, v * ze_b], axis=1).astype(BF16)
        kv = jnp.dot(k.T.astype(BF16), vz, preferred_element_type=F32)
        sf_ref[n] = s
        sb_ref[n] = kv[:, dh:]
        return g_full * s + kv[:, :dh]

    lax.fori_loop(0, nc, p1, s_f0, unroll=RET_UNROLL)

    def p1b(i, s):
        n = nc - 1 - i
        kvb = sb_ref[n]
        sb_ref[n] = s
        return g_full * s + kvb

    lax.fori_loop(0, nc, p1b, s_b0)

    def p2(n, carry):
        r = pl.multiple_of(n * C, C)
        q = qr_ref[pl.ds(r, C), :]
        kb = kr_ref[pl.ds(r, C), :]
        vb = v_ref[0, pl.ds(r, C), :].astype(BF16)
        sc = lax.dot_general(q.astype(BF16), kb, NT_DIMS, preferred_element_type=F32) * dm_ref[0]
        o = jnp.dot(sc.astype(BF16), vb, preferred_element_type=F32)
        qx = jnp.concatenate([q * xi_f, q * xi_b], axis=1).astype(BF16)
        st = jnp.concatenate([sf_ref[n], sb_ref[n]], axis=0).astype(BF16)
        o = o + jnp.dot(qx, st, preferred_element_type=F32)
        o = o * lax.rsqrt(jnp.mean(o * o, axis=-1, keepdims=True) + EPS)
        y_ref[0, pl.ds(r, C), :] = (o * _silu(g_ref[0, pl.ds(r, C), :])).astype(y_ref.dtype)
        return carry

    lax.fori_loop(0, nc, p2, 0, unroll=RET_UNROLL)


def _ret_tables(t, c_len, dh):
    C = RET_CHUNK
    hh = jnp.arange(RET_HEADS, dtype=F32)
    log_g = jnp.log1p(-jnp.exp2(-(5.0 + 2.0 * hh)))
    idx = jnp.arange(C, dtype=F32)
    dm = jnp.exp(jnp.abs(idx[:, None] - idx[None, :])[None] * log_g[:, None, None])

    def lanes(e):
        return jnp.broadcast_to(jnp.exp(e[None, :] * log_g[:, None])[:, :, None], (RET_HEADS, e.shape[0], dh))

    rt = jnp.stack([lanes(idx + 1.0), lanes(C - idx), lanes(C - 1.0 - idx), lanes(idx)], axis=1)
    pos = jnp.arange(c_len, dtype=F32)
    cwt = jnp.stack([lanes(c_len - 1.0 - pos), lanes(pos)], axis=1)
    gc = jnp.broadcast_to(jnp.exp(C * log_g)[:, None, None], (RET_HEADS, SUBLANES, dh))
    rows = t // GRID_W
    row = jnp.repeat(jnp.arange(rows, dtype=F32), GRID_W)
    col = jnp.tile(jnp.arange(GRID_W, dtype=F32), rows)
    q4 = dh // 4
    freqs = ROPE_BASE ** (-jnp.arange(q4, dtype=F32) / q4)
    ang_r = row[:, None] * freqs[None, :]
    ang_c = col[:, None] * freqs[None, :]
    cos = jnp.concatenate([jnp.cos(ang_r)] * 2 + [jnp.cos(ang_c)] * 2, axis=1)
    sin = jnp.concatenate([-jnp.sin(ang_r), jnp.sin(ang_r), -jnp.sin(ang_c), jnp.sin(ang_c)], axis=1)
    return dm, rt, cwt, gc, cos, sin


def _ret_call(qkvg, qkvg_c):
    b, t, w = qkvg.shape
    c_len = qkvg_c.shape[1]
    dh = w // (4 * RET_HEADS)
    nc = t // RET_CHUNK
    dm, rt, cwt, gc, cos, sin = _ret_tables(t, c_len, dh)
    H = RET_HEADS

    def col(off):
        return lambda bi, h: (bi, 0, off + h)

    return pl.pallas_call(
        functools.partial(_ret_kernel, nc=nc, scale=dh ** -0.5),
        grid=(b, H),
        in_specs=[
            pl.BlockSpec((1, t, dh), col(0)),
            pl.BlockSpec((1, t, dh), col(H)),
            pl.BlockSpec((1, t, dh), col(2 * H)),
            pl.BlockSpec((1, t, dh), col(3 * H)),
            pl.BlockSpec((1, c_len, dh), col(H)),
            pl.BlockSpec((1, c_len, dh), col(2 * H)),
            pl.BlockSpec((t, dh), lambda bi, h: (0, 0)),
            pl.BlockSpec((t, dh), lambda bi, h: (0, 0)),
            pl.BlockSpec((1, RET_CHUNK, RET_CHUNK), lambda bi, h: (h, 0, 0)),
            pl.BlockSpec((1, 4, RET_CHUNK, dh), lambda bi, h: (h, 0, 0, 0)),
            pl.BlockSpec((1, 2, c_len, dh), lambda bi, h: (h, 0, 0, 0)),
            pl.BlockSpec((1, SUBLANES, dh), lambda bi, h: (h, 0, 0)),
        ],
        out_specs=pl.BlockSpec((1, t, dh), col(0)),
        out_shape=jax.ShapeDtypeStruct((b, t, H * dh), BF16),
        scratch_shapes=[
            pltpu.VMEM((t, dh), F32),
            pltpu.VMEM((t, dh), BF16),
            pltpu.VMEM((nc, dh, dh), F32),
            pltpu.VMEM((nc, dh, dh), F32),
        ],
        compiler_params=_cparams("parallel", "arbitrary"),
        name="retention",
    )(qkvg, qkvg, qkvg, qkvg, qkvg_c, qkvg_c, cos, sin, dm, rt, cwt, gc)


def _route(logits_t, bias):
    e, n = logits_t.shape
    per = e // N_GROUPS
    s = jax.nn.sigmoid(logits_t)
    sb = s + bias
    neg = -jnp.inf
    g3 = sb.reshape(N_GROUPS, per, n)
    it3 = lax.broadcasted_iota(jnp.int32, g3.shape, 1)
    m1 = jnp.max(g3, axis=1, keepdims=True)
    i1 = jnp.min(jnp.where(g3 == m1, it3, per), axis=1, keepdims=True)
    m2 = jnp.max(jnp.where(it3 == i1, neg, g3), axis=1, keepdims=True)
    gs = (m1 + m2).reshape(N_GROUPS, n)
    itg = lax.broadcasted_iota(jnp.int32, gs.shape, 0)
    gsel = jnp.zeros(gs.shape, jnp.bool_)
    for _ in range(TOP_GROUPS):
        m = jnp.max(gs, axis=0, keepdims=True)
        i = jnp.min(jnp.where(gs == m, itg, N_GROUPS), axis=0, keepdims=True)
        hit = itg == i
        gsel = gsel | hit
        gs = jnp.where(hit, neg, gs)
    gmask = jnp.broadcast_to(gsel.reshape(N_GROUPS, 1, n), (N_GROUPS, per, n)).reshape(e, n)
    v = jnp.where(gmask, sb, neg)
    ite = lax.broadcasted_iota(jnp.int32, v.shape, 0)
    sel = jnp.zeros(v.shape, jnp.bool_)
    for _ in range(TOP_K):
        m = jnp.max(v, axis=0, keepdims=True)
        i = jnp.min(jnp.where(v == m, ite, e), axis=0, keepdims=True)
        hit = ite == i
        sel = sel | hit
        v = jnp.where(hit, neg, v)
    picked = jnp.where(sel, s, 0.0)
    return picked / jnp.sum(picked, axis=0, keepdims=True) * ROUTE_SCALE


def _outproj_kernel(yl_ref, yr_ref, x_ref, g1_ref, sh_ref, sc_ref, nf_ref, wo_ref, wr_ref, rb_ref,
                    x1_ref, h2_ref, gt_ref, cnt_ref):
    dl = yl_ref.shape[2]
    ts = gt_ref.shape[1] // cnt_ref.shape[0]
    w = wr_ref[...]
    w_hi = w.astype(BF16)
    r1 = w - w_hi.astype(F32)
    w_mid = r1.astype(BF16)
    w_lo = (r1 - w_mid.astype(F32)).astype(BF16)
    for k in range(cnt_ref.shape[0]):
        rows = pl.ds(k * ts, ts)
        mx = jnp.dot(yl_ref[0, rows, :], wo_ref[0:dl, :], preferred_element_type=F32)
        mx = mx + jnp.dot(yr_ref[0, rows, :], wo_ref[dl:, :], preferred_element_type=F32)
        x1 = x_ref[0, rows, :] + g1_ref[0] * mx
        x1_ref[0, rows, :] = x1
        h = _rms_mod(x1, nf_ref[...], sh_ref[0], sc_ref[0])
        h_hi = h.astype(BF16)
        h2_ref[0, rows, :] = h_hi
        h_lo = (h - h_hi.astype(F32)).astype(BF16)
        small = (_dot32(h_hi, w_lo) + _dot32(h_lo, w_mid)) + (_dot32(h_hi, w_mid) + _dot32(h_lo, w_hi))
        logits = small + _dot32(h_hi, w_hi)
        ne = logits.shape[1]
        logits_t = jnp.concatenate([logits, jnp.zeros((ts, LANES - ne), F32)], axis=1).T[:ne, :]
        gates = _route(logits_t, rb_ref[...])
        gt_ref[:, rows] = gates
        hit = (gates > 0.0).astype(BF16)
        cnt_ref[k] = lax.dot_general(jnp.ones((SUBLANES, ts), BF16), hit, NT_DIMS, preferred_element_type=F32)


def _outproj_call(y_lru, y_ret, x, g1, sh2, sc2, norm_ffn, w_out_bf, w_router, b_router):
    b, t, d = x.shape
    dl = y_lru.shape[2]
    e = w_router.shape[1]
    ts = min(t, MOE_TILE)
    tt = min(t, OUTPROJ_TILES * ts)
    nt = t // tt
    per = tt // ts
    mod_map = lambda bi, i: (bi, 0, 0)
    return pl.pallas_call(
        _outproj_kernel,
        grid=(b, nt),
        in_specs=[
            pl.BlockSpec((1, tt, dl), lambda bi, i: (bi, i, 0)),
            pl.BlockSpec((1, tt, d - dl), lambda bi, i: (bi, i, 0)),
            pl.BlockSpec((1, tt, d), lambda bi, i: (bi, i, 0)),
            pl.BlockSpec((1, 1, d), mod_map),
            pl.BlockSpec((1, 1, d), mod_map),
            pl.BlockSpec((1, 1, d), mod_map),
            pl.BlockSpec((1, d), lambda bi, i: (0, 0)),
            pl.BlockSpec((d, d), lambda bi, i: (0, 0)),
            pl.BlockSpec((d, e), lambda bi, i: (0, 0)),
            pl.BlockSpec((e, 1), lambda bi, i: (0, 0)),
        ],
        out_specs=[
            pl.BlockSpec((1, tt, d), lambda bi, i: (bi, i, 0)),
            pl.BlockSpec((1, tt, d), lambda bi, i: (bi, i, 0)),
            pl.BlockSpec((e, tt), lambda bi, i: (0, bi * nt + i)),
            pl.BlockSpec((per, SUBLANES, e), lambda bi, i: (bi * nt + i, 0, 0)),
        ],
        out_shape=[
            jax.ShapeDtypeStruct((b, t, d), F32),
            jax.ShapeDtypeStruct((b, t, d), BF16),
            jax.ShapeDtypeStruct((e, b * t), F32),
            jax.ShapeDtypeStruct((b * nt * per, SUBLANES, e), F32),
        ],
        compiler_params=_cparams("parallel", "arbitrary"),
        name="outproj_route",
    )(y_lru, y_ret, x, g1, sh2, sc2, norm_ffn.reshape(1, d), w_out_bf, w_router, b_router.reshape(e, 1))


MOE_TILE = 256
MOE_CHUNK = 16
MOE_RB = 512
MOE_BM = 1024
OUTPROJ_TILES = 2
NO_RANK = 512.0

TN_DIMS = (((0,), (0,)), ((), ()))
TAB_LOFF, TAB_GBASE, TAB_CPAD, TAB_NROWS, TAB_EEND = range(5)


def _round_up(v, m):
    return -(-v // m) * m


def _tables_kernel(cnt_ref, loff_ref, gbase_ref, cpad_ref, eend_ref, blk_ref, nused_ref, ctab_ref, *, nb):
    cnt = cnt_ref[...]
    ns, ne = cnt.shape
    cpad = jnp.ceil(cnt * (1.0 / MOE_CHUNK)) * MOE_CHUNK
    upper = (lax.broadcasted_iota(jnp.int32, (ne, ne), 0) < lax.broadcasted_iota(jnp.int32, (ne, ne), 1)).astype(F32)
    lower = (lax.broadcasted_iota(jnp.int32, (ns, ns), 1) < lax.broadcasted_iota(jnp.int32, (ns, ns), 0)).astype(F32)
    loff = jnp.dot(cpad, upper, precision=HIGHEST, preferred_element_type=F32)
    before = jnp.dot(lower, cpad, precision=HIGHEST, preferred_element_type=F32)
    tot = jnp.sum(cpad, axis=0, keepdims=True)
    epad = jnp.ceil(tot * (1.0 / MOE_BM)) * MOE_BM
    epad8 = jnp.broadcast_to(epad, (SUBLANES, ne))
    ebase = jnp.dot(epad8, upper, precision=HIGHEST, preferred_element_type=F32)[0:1]
    eend = ebase + epad
    loff_ref[...] = loff
    gbase_ref[...] = ebase + before
    cpad_ref[...] = cpad
    eend_ref[...] = eend
    start = lax.broadcasted_iota(jnp.int32, (nb, ne), 0).astype(F32) * MOE_BM
    blk = jnp.sum((eend <= start).astype(F32), axis=1, keepdims=True)
    blk_ref[...] = jnp.minimum(blk, ne - 1.0)
    nused_ref[...] = jnp.sum(epad, axis=1, keepdims=True) * (1.0 / MOE_BM)
    nch = ctab_ref.shape[1]
    cidx = lax.broadcasted_iota(jnp.int32, (ns, nch), 1).astype(F32)
    shift = (ebase + before - loff) * (1.0 / MOE_CHUNK)
    ctab = jnp.full((ns, nch), -1.0, F32)
    for e in range(ne):
        lo = loff[:, e:e + 1] * (1.0 / MOE_CHUNK)
        hi = lo + cpad[:, e:e + 1] * (1.0 / MOE_CHUNK)
        ctab = jnp.where((cidx >= lo) & (cidx < hi), cidx + shift[:, e:e + 1], ctab)
    ctab_ref[...] = ctab


def _tables_call(cnt, nb, nch):
    ns, ne = cnt.shape
    f = lambda shape: jax.ShapeDtypeStruct(shape, F32)
    return pl.pallas_call(
        functools.partial(_tables_kernel, nb=nb),
        out_shape=[f((ns, ne)), f((ns, ne)), f((ns, ne)), f((1, ne)), f((nb, 1)), f((1, 1)), f((ns, nch))],
        compiler_params=pltpu.CompilerParams(vmem_limit_bytes=VMEM_LIMIT_BYTES),
        name="moe_tables",
    )(cnt)


def _tile_rank_rhs(gt, lcol):
    ne, ts = gt.shape
    hit = gt > 0.0
    before = (lax.broadcasted_iota(jnp.int32, (ts, ts), 0) < lax.broadcasted_iota(jnp.int32, (ts, ts), 1)).astype(BF16)
    rank = jnp.dot(hit.astype(BF16), before, preferred_element_type=F32)
    rm = jnp.where(hit, rank, NO_RANK)
    q = jnp.broadcast_to(lcol * (1.0 / MOE_CHUNK), (ne, LANES))
    return jnp.concatenate([rm, q], axis=1).astype(BF16)


def _onehot_rows(r0, rhs, lrow, crow, ts):
    ne = lrow.shape[1]
    rows = (r0 + lax.broadcasted_iota(jnp.int32, (MOE_RB, ne), 0)).astype(F32)
    seg = ((rows >= lrow) & (rows < crow)).astype(BF16)
    ex = jnp.dot(seg, rhs, preferred_element_type=F32)
    rowl = (r0 + lax.broadcasted_iota(jnp.int32, (MOE_RB, LANES), 0)).astype(F32)
    tgt = rowl - MOE_CHUNK * ex[:, ts:ts + LANES]
    p = ex[:, :ts] == jnp.concatenate([tgt] * (ts // LANES), axis=1)
    return p, (ex[:, ts + LANES:] if rhs.shape[1] > ts + LANES else None)


def _chunk_copy(buf, slot, c, row, hbm, sem, to_hbm):
    loc = buf.at[slot, pl.ds(c * MOE_CHUNK, MOE_CHUNK)]
    glob = hbm.at[pl.ds(pl.multiple_of(row, MOE_CHUNK), MOE_CHUNK)]
    return pltpu.make_async_copy(loc, glob, sem.at[slot]) if to_hbm else pltpu.make_async_copy(glob, loc, sem.at[slot])


def _wait_chunks(n, buf, slot, hbm, sem, to_hbm):
    for _ in range(n):
        _chunk_copy(buf, slot, 0, 0, hbm, sem, to_hbm).wait()


def _wait_chunks_dyn(n, buf, slot, hbm, sem, to_hbm):
    def body(i, c):
        _chunk_copy(buf, slot, 0, 0, hbm, sem, to_hbm).wait()
        return c

    lax.fori_loop(0, n, body, 0)


def _dispatch_kernel(tab_ref, ctab_ref, h_ref, gt_ref, lrow_ref, crow_ref, lcol_ref, xs_hbm, xbuf, zbuf, sem,
                     *, ns, trash0):
    s = pl.program_id(0)
    slot = s % 2
    ts = h_ref.shape[0]
    ne = gt_ref.shape[0]
    rows_max = xbuf.shape[1]
    nch = rows_max // MOE_CHUNK

    @pl.when(s >= 2)
    def _():
        _wait_chunks(nch, xbuf, slot, xs_hbm, sem, True)

    rhs = _tile_rank_rhs(gt_ref[...], lcol_ref[0])
    for r0 in range(0, rows_max, MOE_RB):
        p, _ = _onehot_rows(r0, rhs, lrow_ref[0], crow_ref[0], ts)
        xbuf[slot, r0:r0 + MOE_RB, :] = jnp.dot(p.astype(BF16), h_ref[...],
                                                preferred_element_type=F32).astype(xbuf.dtype)
    for c in range(nch):
        g = ctab_ref[0, 0, c]
        row = jnp.where(g >= 0, g * MOE_CHUNK, trash0 + slot * rows_max + c * MOE_CHUNK)
        _chunk_copy(xbuf, slot, c, row, xs_hbm, sem, True).start()

    @pl.when(s == ns - 1)
    def _():
        zbuf[...] = jnp.zeros(zbuf.shape, zbuf.dtype)

        def per_expert(e, total):
            first = tab_ref[0, TAB_GBASE, e] + tab_ref[0, TAB_CPAD, e]
            ntail = lax.shift_right_logical(tab_ref[0, TAB_EEND, e] - first, MOE_CHUNK.bit_length() - 1)

            def per_chunk(j, c2):
                dst = xs_hbm.at[pl.ds(pl.multiple_of(first + j * MOE_CHUNK, MOE_CHUNK), MOE_CHUNK)]
                pltpu.make_async_copy(zbuf, dst, sem.at[slot]).start()
                return c2

            lax.fori_loop(0, ntail, per_chunk, 0)
            return total + ntail

        ntail = lax.fori_loop(0, ne, per_expert, 0)
        _wait_chunks_dyn(ntail, xbuf, slot, xs_hbm, sem, True)
        _wait_chunks(nch, xbuf, slot, xs_hbm, sem, True)
        if ns >= 2:
            _wait_chunks(nch, xbuf, 1 - slot, xs_hbm, sem, True)


def _dispatch_call(tab, ctab, h2, gt, lrow, crow, lcol, p_max, rows_max):
    n, d = h2.shape
    ne = gt.shape[0]
    ts = min(n, MOE_TILE)
    ns = n // ts
    nch = rows_max // MOE_CHUNK
    return pl.pallas_call(
        functools.partial(_dispatch_kernel, ns=ns, trash0=p_max),
        grid=(ns,),
        in_specs=[
            pl.BlockSpec((1, 5, ne), lambda s: (s, 0, 0), memory_space=pltpu.SMEM),
            pl.BlockSpec((1, 1, nch), lambda s: (s, 0, 0), memory_space=pltpu.SMEM),
            pl.BlockSpec((ts, d), lambda s: (s, 0)),
            pl.BlockSpec((ne, ts), lambda s: (0, s)),
            pl.BlockSpec((1, 1, ne), lambda s: (s, 0, 0)),
            pl.BlockSpec((1, 1, ne), lambda s: (s, 0, 0)),
            pl.BlockSpec((1, ne, 1), lambda s: (s, 0, 0)),
        ],
        out_specs=pl.BlockSpec(memory_space=pl.ANY),
        out_shape=jax.ShapeDtypeStruct((p_max + 2 * rows_max, d), BF16),
        scratch_shapes=[
            pltpu.VMEM((2, rows_max, d), BF16),
            pltpu.VMEM((MOE_CHUNK, d), BF16),
            pltpu.SemaphoreType.DMA((2,)),
        ],
        compiler_params=_cparams("arbitrary"),
        name="moe_dispatch",
    )(tab, ctab, h2, gt, lrow, crow, lcol)


def _experts_kernel(blk_ref, nused_ref, x_ref, w1_ref, w3_ref, w2_ref, y_ref):
    del blk_ref

    @pl.when(pl.program_id(0) < nused_ref[0])
    def _():
        xv = x_ref[...]
        hid = _silu(jnp.dot(xv, w1_ref[0], preferred_element_type=F32)) * jnp.dot(xv, w3_ref[0], preferred_element_type=F32)
        y_ref[...] = jnp.dot(hid.astype(BF16), w2_ref[0], preferred_element_type=F32).astype(y_ref.dtype)


def _experts_call(blk, nused, xs, w1, w3, w2):
    d = xs.shape[1]
    _, _, de = w1.shape
    nb = blk.shape[0]
    p_max = nb * MOE_BM
    row_map = lambda i, blk_r, nu_r: (jnp.maximum(jnp.minimum(i, nu_r[0] - 1), 0), 0)
    w_map = lambda i, blk_r, nu_r: (blk_r[i], 0, 0)
    return pl.pallas_call(
        _experts_kernel,
        grid_spec=pltpu.PrefetchScalarGridSpec(
            num_scalar_prefetch=2,
            grid=(nb,),
            in_specs=[
                pl.BlockSpec((MOE_BM, d), row_map),
                pl.BlockSpec((1, d, de), w_map),
                pl.BlockSpec((1, d, de), w_map),
                pl.BlockSpec((1, de, d), w_map),
            ],
            out_specs=pl.BlockSpec((MOE_BM, d), row_map),
        ),
        out_shape=jax.ShapeDtypeStruct((p_max, d), BF16),
        compiler_params=_cparams("arbitrary"),
        name="moe_experts",
    )(blk, nused, xs, w1, w3, w2)


def _combine_kernel(ctab_ref, ctabn_ref, gt_ref, lrow_ref, crow_ref, lcol_ref, h_ref, x1_ref, g2_ref,
                    s1_ref, s3_ref, s2_ref, nfin_ref, ys_hbm, o_ref, ybuf, sem, *, ns):
    s = pl.program_id(0)
    slot = s % 2
    ts = h_ref.shape[0]
    rows_max = ybuf.shape[1]
    nch = rows_max // MOE_CHUNK

    def gather(tab, sl):
        for c in range(nch):
            row = jnp.maximum(tab[0, 0, c], 0) * MOE_CHUNK
            _chunk_copy(ybuf, sl, c, row, ys_hbm, sem, False).start()

    @pl.when(s == 0)
    def _():
        gather(ctab_ref, 0)

    gather(ctabn_ref, 1 - slot)

    h = h_ref[...]
    hs = _silu(jnp.dot(h, s1_ref[...], preferred_element_type=F32)) * jnp.dot(h, s3_ref[...], preferred_element_type=F32)
    acc = jnp.dot(hs.astype(BF16), s2_ref[...], preferred_element_type=F32)

    gt = gt_ref[...]
    rhs = jnp.concatenate([_tile_rank_rhs(gt, lcol_ref[0]), gt.astype(BF16)], axis=1)
    _wait_chunks(nch, ybuf, slot, ys_hbm, sem, False)
    for r0 in range(0, rows_max, MOE_RB):
        p, gexp = _onehot_rows(r0, rhs, lrow_ref[0], crow_ref[0], ts)
        gm = jnp.where(p, gexp, 0.0).astype(BF16)
        acc = acc + lax.dot_general(gm, ybuf[slot, r0:r0 + MOE_RB, :], TN_DIMS, preferred_element_type=F32)
    xo = x1_ref[...] + g2_ref[0] * acc
    o_ref[...] = xo * lax.rsqrt(jnp.mean(xo * xo, axis=-1, keepdims=True) + EPS) * nfin_ref[...]

    @pl.when(s == ns - 1)
    def _():
        _wait_chunks(nch, ybuf, 1 - slot, ys_hbm, sem, False)


def _combine_call(ctab, gt, lrow, crow, lcol, h2, x1, g2, ws1, ws3, ws2, norm_final, ys, rows_max, t):
    n, d = h2.shape
    ne = gt.shape[0]
    ts = min(n, MOE_TILE)
    ns = n // ts
    per_b = t // ts
    nch = rows_max // MOE_CHUNK
    return pl.pallas_call(
        functools.partial(_combine_kernel, ns=ns),
        grid=(ns,),
        in_specs=[
            pl.BlockSpec((1, 1, nch), lambda s: (s, 0, 0), memory_space=pltpu.SMEM),
            pl.BlockSpec((1, 1, nch), lambda s: (jnp.minimum(s + 1, ns - 1), 0, 0), memory_space=pltpu.SMEM),
            pl.BlockSpec((ne, ts), lambda s: (0, s)),
            pl.BlockSpec((1, 1, ne), lambda s: (s, 0, 0)),
            pl.BlockSpec((1, 1, ne), lambda s: (s, 0, 0)),
            pl.BlockSpec((1, ne, 1), lambda s: (s, 0, 0)),
            pl.BlockSpec((ts, d), lambda s: (s, 0)),
            pl.BlockSpec((ts, d), lambda s: (s, 0)),
            pl.BlockSpec((1, 1, d), lambda s: (s // per_b, 0, 0)),
            pl.BlockSpec(ws1.shape, lambda s: (0, 0)),
            pl.BlockSpec(ws3.shape, lambda s: (0, 0)),
            pl.BlockSpec(ws2.shape, lambda s: (0, 0)),
            pl.BlockSpec((1, d), lambda s: (0, 0)),
            pl.BlockSpec(memory_space=pl.ANY),
        ],
        out_specs=pl.BlockSpec((ts, d), lambda s: (s, 0)),
        out_shape=jax.ShapeDtypeStruct((n, d), F32),
        scratch_shapes=[
            pltpu.VMEM((2, rows_max, d), BF16),
            pltpu.SemaphoreType.DMA((2,)),
        ],
        compiler_params=_cparams("arbitrary"),
        name="moe_combine",
    )(ctab, ctab, gt, lrow, crow, lcol, h2, x1, g2, ws1, ws3, ws2, norm_final.reshape(1, d), ys)


def _moe_call(h2, gt, cnt, x1, g2, w1, w3, w2, ws1, ws3, ws2, norm_final, t):
    n, d = h2.shape
    ne = gt.shape[0]
    ts = min(n, MOE_TILE)
    ns = n // ts
    rows_max = _round_up(TOP_K * ts + ne * (MOE_CHUNK - 1), MOE_RB)
    p_max = _round_up(n * TOP_K + ns * ne * (MOE_CHUNK - 1) + ne * (MOE_BM - 1), MOE_BM)
    nb = p_max // MOE_BM

    nch = rows_max // MOE_CHUNK
    loff, gbase, cpad, eend, blk, nused, ctab = _tables_call(cnt[:, 0, :], nb, nch)
    nrows = loff[:, ne - 1:] + cpad[:, ne - 1:]
    tab = jnp.stack([loff, gbase, cpad, jnp.broadcast_to(nrows, (ns, ne)), jnp.broadcast_to(eend, (ns, ne))],
                    axis=1).astype(jnp.int32)
    ctab = ctab.astype(jnp.int32).reshape(ns, 1, nch)
    lrow = loff.reshape(ns, 1, ne)
    crow = (loff + cpad).reshape(ns, 1, ne)
    lcol = loff.reshape(ns, ne, 1)

    xs = _dispatch_call(tab, ctab, h2, gt, lrow, crow, lcol, p_max, rows_max)
    ys = _experts_call(blk.reshape(nb).astype(jnp.int32), nused.reshape(1).astype(jnp.int32), xs, w1, w3, w2)
    return _combine_call(ctab, gt, lrow, crow, lcol, h2, x1, g2, ws1, ws3, ws2, norm_final, ys, rows_max, t)


def _block_diag(w):
    nb, n, _ = w.shape
    eye = jnp.eye(nb, dtype=w.dtype)
    return (eye[:, None, :, None] * w[:, :, None, :]).reshape(nb * n, nb * n)


def _lru_gate_weights(w_rec, b_rec, w_inp, b_inp):
    d_lru = b_rec.shape[1]
    half = d_lru // 2
    ws, bs = [], []
    for dirn in range(2):
        wr, wi = _block_diag(w_rec[dirn]), _block_diag(w_inp[dirn])
        ws.append(jnp.stack([jnp.concatenate([wr[s:s + half, s:s + half], wi[s:s + half, s:s + half]], axis=1)
                             for s in (0, half)]))
        bs.append(jnp.stack([jnp.concatenate([b_rec[dirn, s:s + half], b_inp[dirn, s:s + half]])[None, :]
                             for s in (0, half)]))
    return jnp.stack(ws).astype(BF16), jnp.stack(bs).astype(F32)


def kernel(x, c, ctx, c_ctx, w_mod, b_mod, norm_mix, norm_ffn, w_in, conv_w, conv_b, lru_w_rec, lru_b_rec, lru_w_in, lru_b_in, lru_lambda, w_out, w_router, b_router, w_exp_gate, w_exp_up, w_exp_down, w_sh_gate, w_sh_up, w_sh_down, norm_final):
    b, t, d = x.shape
    depth = w_mod.shape[0]
    assert depth == 1, "context-stream update for deeper stacks is not implemented"
    l = 0
    d_lru = conv_w.shape[2]
    n_xg = 2 * d_lru

    rows = -(-(b + 1) // SUBLANES) * SUBLANES
    cc = jnp.concatenate([c, c_ctx[None, :], jnp.zeros((rows - b - 1, d), F32)], axis=0)
    mod = _mod_call(cc, w_mod[l], b_mod[l]).reshape(rows, 1, -1)
    sh1, sc1, g1, sh2, sc2, g2 = (mod[:, :, i * d:(i + 1) * d] for i in range(6))

    w_in_bf = w_in[l].astype(BF16)
    gain_mix = norm_mix[l].reshape(1, d)
    xg, qkvg = _inproj_call(x, sh1[:b], sc1[:b], gain_mix, w_in_bf, n_xg, True)
    xg_c, qkvg_c = _inproj_call(ctx, sh1[b:b + 1], sc1[b:b + 1], gain_mix, w_in_bf, n_xg, False)

    wg, bg = _lru_gate_weights(lru_w_rec[l], lru_b_rec[l], lru_w_in[l], lru_b_in[l])
    y_lru = _lru_call(xg, xg_c, conv_w[l], conv_b[l], wg, bg, lru_lambda[l])
    y_ret = _ret_call(qkvg, qkvg_c)

    x1, h2, gt, cnt = _outproj_call(y_lru, y_ret, x, g1[:b], sh2[:b], sc2[:b], norm_ffn[l], w_out[l].astype(BF16),
                                    w_router[l], b_router[l])

    out = _moe_call(h2.reshape(b * t, d), gt, cnt, x1.reshape(b * t, d), g2[:b],
                    w_exp_gate[l].astype(BF16), w_exp_up[l].astype(BF16), w_exp_down[l].astype(BF16),
                    w_sh_gate[l].astype(BF16), w_sh_up[l].astype(BF16), w_sh_down[l].astype(BF16), norm_final, t)
    return out.reshape(b, t, d)
```

```python
import functools

import jax
import jax.numpy as jnp
from jax import lax
from jax.experimental import pallas as pl
from jax.experimental.pallas import tpu as pltpu

F32 = jnp.float32
BF16 = jnp.bfloat16
HIGHEST = lax.Precision.HIGHEST

EPS = 1e-6
GRID_W = 64
LRU_C = 8.0
RET_HEADS = 4
RET_CHUNK = 128
RET_UNROLL = 8
ROPE_BASE = 10000.0
N_GROUPS = 8
TOP_GROUPS = 4
TOP_K = 6
ROUTE_SCALE = 2.5

LANES = 128
SUBLANES = 8
VMEM_LIMIT_BYTES = 56 * 1024 * 1024

NT_DIMS = (((1,), (1,)), ((), ()))


def _cparams(*sem):
    return pltpu.CompilerParams(dimension_semantics=sem, vmem_limit_bytes=VMEM_LIMIT_BYTES)


def _silu(v):
    return v * jax.nn.sigmoid(v)


def _dot32(a, b):
    return jnp.dot(a, b, preferred_element_type=F32)


def _rms_mod(x, gain, shift, scale):
    y = x * lax.rsqrt(jnp.mean(x * x, axis=-1, keepdims=True) + EPS) * gain
    return y * (1.0 + scale) + shift


def _mod_kernel(c_ref, w_ref, b_ref, o_ref):
    s = _silu(c_ref[...])
    o_ref[...] = jnp.dot(s, w_ref[...], precision=HIGHEST, preferred_element_type=F32) + b_ref[...]


def _mod_call(cc, w_mod, b_mod):
    rows, d = cc.shape
    n = w_mod.shape[1]
    tn = 512
    return pl.pallas_call(
        _mod_kernel,
        grid=(n // tn,),
        in_specs=[
            pl.BlockSpec((rows, d), lambda j: (0, 0)),
            pl.BlockSpec((d, tn), lambda j: (0, j)),
            pl.BlockSpec((1, tn), lambda j: (0, j)),
        ],
        out_specs=pl.BlockSpec((rows, tn), lambda j: (0, j)),
        out_shape=jax.ShapeDtypeStruct((rows, n), F32),
        compiler_params=_cparams("arbitrary"),
        name="mod",
    )(cc, w_mod, b_mod.reshape(1, n))


def _inproj_kernel(x_ref, sh_ref, sc_ref, g_ref, w_ref, xg_ref, qkvg_ref, *, n_xg):
    h = _rms_mod(x_ref[0], g_ref[...], sh_ref[0], sc_ref[0]).astype(BF16)
    xg_ref[0] = jnp.dot(h, w_ref[:, :n_xg], preferred_element_type=F32)
    qkvg_ref[0] = jnp.dot(h, w_ref[:, n_xg:], preferred_element_type=F32)


def _inproj_call(x, shift, scale, gain, w_in_bf, n_xg, per_batch_mod):
    b, t, d = x.shape
    n_all = w_in_bf.shape[1]
    tt = min(t, 512)
    mod_map = (lambda bi, i: (bi, 0, 0)) if per_batch_mod else (lambda bi, i: (0, 0, 0))
    return pl.pallas_call(
        functools.partial(_inproj_kernel, n_xg=n_xg),
        grid=(b, t // tt),
        in_specs=[
            pl.BlockSpec((1, tt, d), lambda bi, i: (bi, i, 0)),
            pl.BlockSpec((1, 1, d), mod_map),
            pl.BlockSpec((1, 1, d), mod_map),
            pl.BlockSpec((1, d), lambda bi, i: (0, 0)),
            pl.BlockSpec((d, n_all), lambda bi, i: (0, 0)),
        ],
        out_specs=[
            pl.BlockSpec((1, tt, n_xg), lambda bi, i: (bi, i, 0)),
            pl.BlockSpec((1, tt, n_all - n_xg), lambda bi, i: (bi, i, 0)),
        ],
        out_shape=[
            jax.ShapeDtypeStruct((b, t, n_xg), F32),
            jax.ShapeDtypeStruct((b, t, n_all - n_xg), F32),
        ],
        compiler_params=_cparams("parallel", "arbitrary"),
        name="inproj",
    )(x, shift, scale, gain, w_in_bf)


SCAN_ROWS = 64


def _tile_scan(a, b, reverse):
    row = lax.broadcasted_iota(jnp.int32, a.shape, 0)
    for s in (1, 2, 4):
        if reverse:
            a_s = pltpu.roll(a, SUBLANES - s, 0)
            b_s = pltpu.roll(b, SUBLANES - s, 0)
            m = row < SUBLANES - s
        else:
            a_s = pltpu.roll(a, s, 0)
            b_s = pltpu.roll(b, s, 0)
            m = row >= s
        b = jnp.where(m, a * b_s + b, b)
        a = jnp.where(m, a * a_s, a)
    return a, b


def _lru_kernel(xa_ref, ga_ref, xac_ref, cw_ref, cb_ref, wg_ref, bg_ref, lam_ref, y_ref,
                hf_ref, u_ref, xe_ref, a_ref, b_ref, carry_ref, *, t_len, c_len, tc, nc):
    j = pl.program_id(1)
    half = wg_ref.shape[2]

    def conv(n):
        u = cb_ref[...]
        for k in range(cw_ref.shape[0]):
            u = u + cw_ref[k:k + 1, :] * xe_ref[6 + k:6 + k + n, :]
        return u

    def gates(u, n, dirn):
        lam = -lam_ref[dirn:dirn + 1, :]
        sp = jnp.maximum(lam, 0.0) + jnp.log1p(jnp.exp(-jnp.abs(lam)))
        for hf in range(2):
            uh = u[:, hf * half:(hf + 1) * half]
            z = jnp.dot(uh.astype(BF16), wg_ref[dirn, hf], preferred_element_type=F32) + bg_ref[dirn, hf]
            r = jax.nn.sigmoid(z[:, :half])
            i = jax.nn.sigmoid(z[:, half:])
            log_a = -LRU_C * r * sp[:, hf * half:(hf + 1) * half]
            a = jnp.exp(log_a)
            a_ref[0:n, hf * half:(hf + 1) * half] = a
            b_ref[0:n, hf * half:(hf + 1) * half] = jnp.sqrt((1.0 - a) * (1.0 + a)) * (i * uh)

    def scan(n, dirn, emit):
        ng = n // SCAN_ROWS
        nt = SCAN_ROWS // SUBLANES

        def body(gi, carry):
            g = gi if dirn == 0 else ng - 1 - gi
            base = pl.multiple_of(g * SCAN_ROWS, SCAN_ROWS)
            av = a_ref[pl.ds(base, SCAN_ROWS), :]
            bv = b_ref[pl.ds(base, SCAN_ROWS), :]
            hs = [None] * nt
            order = range(nt) if dirn == 0 else range(nt - 1, -1, -1)
            for k in order:
                a, b = _tile_scan(av[k * SUBLANES:(k + 1) * SUBLANES], bv[k * SUBLANES:(k + 1) * SUBLANES],
                                  dirn == 1)
                h = a * carry + b
                carry = h[SUBLANES - 1:SUBLANES] if dirn == 0 else h[0:1]
                hs[k] = h
            if emit is not None:
                emit(base, jnp.concatenate(hs, axis=0))
            return carry

        carry_ref[...] = lax.fori_loop(0, ng, body, carry_ref[...])

    def run(dirn):
        first = (j == 0) if dirn == 0 else (j == nc)
        ci = j if dirn == 0 else 2 * nc - 1 - j

        @pl.when(first)
        def _():
            zeros = jnp.zeros((SUBLANES, xe_ref.shape[1]), F32)
            xe_ref[0:SUBLANES] = zeros
            xe_ref[SUBLANES:SUBLANES + c_len] = xac_ref[0]
            xe_ref[SUBLANES + c_len:2 * SUBLANES + c_len] = zeros
            gates(conv(c_len), c_len, dirn)
            carry_ref[...] = jnp.zeros(carry_ref.shape, F32)
            scan(c_len, dirn, None)

        r0 = pl.multiple_of(ci * tc, tc)
        if dirn == 0:
            p0 = pl.multiple_of(jnp.maximum(r0 - SUBLANES, 0), SUBLANES)
            n0 = pl.multiple_of(jnp.minimum(r0 + tc, t_len - SUBLANES), SUBLANES)
            prev = xa_ref[0, pl.ds(p0, SUBLANES), :]
            nxt = xa_ref[0, pl.ds(n0, SUBLANES), :]
            xe_ref[0:SUBLANES] = jnp.where(ci > 0, prev, 0.0)
            xe_ref[SUBLANES:SUBLANES + tc] = xa_ref[0, pl.ds(r0, tc), :]
            xe_ref[SUBLANES + tc:2 * SUBLANES + tc] = jnp.where(ci < nc - 1, nxt, 0.0)
            u = conv(tc)
            u_ref[pl.ds(r0, tc), :] = u
        else:
            u = u_ref[pl.ds(r0, tc), :]
        gates(u, tc, dirn)

        if dirn == 0:
            def emit(base, h):
                hf_ref[pl.ds(r0 + base, SCAN_ROWS), :] = h
        else:
            def emit(base, h):
                tot = hf_ref[pl.ds(r0 + base, SCAN_ROWS), :] + h
                ga = ga_ref[0, pl.ds(base, SCAN_ROWS), :]
                y_ref[0, pl.ds(base, SCAN_ROWS), :] = (tot * jax.nn.gelu(ga)).astype(y_ref.dtype)
        scan(tc, dirn, emit)

    @pl.when(j < nc)
    def _():
        run(0)

    @pl.when(j >= nc)
    def _():
        run(1)


def _lru_call(xg, xg_c, conv_w, conv_b, wg, bg, lam):
    b, t, _ = xg.shape
    c_len = xg_c.shape[1]
    d_lru = conv_w.shape[1]
    tc = min(t, 256)
    nc = t // tc
    assert c_len <= tc and c_len % SCAN_ROWS == 0 and tc % SCAN_ROWS == 0

    def chunk_map(bi, j):
        return (bi, jnp.where(j < nc, nc - 1, 2 * nc - 1 - j), 0)

    def ga_map(bi, j):
        return (bi, jnp.where(j < nc, nc - 1, 2 * nc - 1 - j), 1)

    return pl.pallas_call(
        functools.partial(_lru_kernel, t_len=t, c_len=c_len, tc=tc, nc=nc),
        grid=(b, 2 * nc),
        in_specs=[
            pl.BlockSpec((1, t, d_lru), lambda bi, j: (bi, 0, 0)),
            pl.BlockSpec((1, tc, d_lru), ga_map),
            pl.BlockSpec((1, c_len, d_lru), lambda bi, j: (bi, 0, 0)),
            pl.BlockSpec(conv_w.shape, lambda bi, j: (0, 0)),
            pl.BlockSpec((1, d_lru), lambda bi, j: (0, 0)),
            pl.BlockSpec(wg.shape, lambda bi, j: (0, 0, 0, 0)),
            pl.BlockSpec(bg.shape, lambda bi, j: (0, 0, 0, 0)),
            pl.BlockSpec(lam.shape, lambda bi, j: (0, 0)),
        ],
        out_specs=pl.BlockSpec((1, tc, d_lru), chunk_map),
        out_shape=jax.ShapeDtypeStruct((b, t, d_lru), BF16),
        scratch_shapes=[
            pltpu.VMEM((t, d_lru), F32),
            pltpu.VMEM((t, d_lru), F32),
            pltpu.VMEM((tc + 2 * SUBLANES, d_lru), F32),
            pltpu.VMEM((tc, d_lru), F32),
            pltpu.VMEM((tc, d_lru), F32),
            pltpu.VMEM((1, d_lru), F32),
        ],
        compiler_params=_cparams("parallel", "arbitrary"),
        name="lru",
    )(xg, xg, xg_c, conv_w, conv_b.reshape(1, d_lru), wg, bg, lam)


def _ret_kernel(q_ref, k_ref, v_ref, g_ref, kc_ref, vc_ref, cos_ref, sin_ref, dm_ref, rt_ref, cwt_ref, gc_ref,
                y_ref, qr_ref, kr_ref, sf_ref, sb_ref, *, nc, scale):
    C = RET_CHUNK
    dh = q_ref.shape[2]
    def rope(xv, cs, sn):
        return xv * cs + pltpu.roll(xv, dh // 2, 1) * sn

    xi_f, xi_b, ze_f, ze_b = rt_ref[0, 0], rt_ref[0, 1], rt_ref[0, 2], rt_ref[0, 3]
    g_chunk = gc_ref[0]
    g_full = jnp.concatenate([g_chunk] * (dh // SUBLANES), axis=0)

    kc = (kc_ref[0] * scale).T.astype(BF16)
    vc = vc_ref[0]
    s_f0 = jnp.dot(kc, (vc * cwt_ref[0, 0]).astype(BF16), preferred_element_type=F32)
    s_b0 = jnp.dot(kc, (vc * cwt_ref[0, 1]).astype(BF16), preferred_element_type=F32)

    def p1(n, s):
        r = pl.multiple_of(n * C, C)
        cs, sn = cos_ref[pl.ds(r, C), :], sin_ref[pl.ds(r, C), :]
        k = rope(k_ref[0, pl.ds(r, C), :], cs, sn) * scale
        qr_ref[pl.ds(r, C), :] = rope(q_ref[0, pl.ds(r, C), :], cs, sn)
        kr_ref[pl.ds(r, C), :] = k.astype(BF16)
        v = v_ref[0, pl.ds(r, C), :]
        vz = jnp.concatenate([v * ze_f, v * ze_b], axis=1).astype(BF16)
        kv = jnp.dot(k.T.astype(BF16), vz, preferred_element_type=F32)
        sf_ref[n] = s
        sb_ref[n] = kv[:, dh:]
        return g_full * s + kv[:, :dh]

    lax.fori_loop(0, nc, p1, s_f0, unroll=RET_UNROLL)

    def p1b(i, s):
        n = nc - 1 - i
        kvb = sb_ref[n]
        sb_ref[n] = s
        return g_full * s + kvb

    lax.fori_loop(0, nc, p1b, s_b0)

    def p2(n, carry):
        r = pl.multiple_of(n * C, C)
        q = qr_ref[pl.ds(r, C), :]
        kb = kr_ref[pl.ds(r, C), :]
        vb = v_ref[0, pl.ds(r, C), :].astype(BF16)
        sc = lax.dot_general(q.astype(BF16), kb, NT_DIMS, preferred_element_type=F32) * dm_ref[0]
        o = jnp.dot(sc.astype(BF16), vb, preferred_element_type=F32)
        qx = jnp.concatenate([q * xi_f, q * xi_b], axis=1).astype(BF16)
        st = jnp.concatenate([sf_ref[n], sb_ref[n]], axis=0).astype(BF16)
        o = o + jnp.dot(qx, st, preferred_element_type=F32)
        o = o * lax.rsqrt(jnp.mean(o * o, axis=-1, keepdims=True) + EPS)
        y_ref[0, pl.ds(r, C), :] = (o * _silu(g_ref[0, pl.ds(r, C), :])).astype(y_ref.dtype)
        return carry

    lax.fori_loop(0, nc, p2, 0, unroll=RET_UNROLL)


def _ret_tables(t, c_len, dh):
    C = RET_CHUNK
    hh = jnp.arange(RET_HEADS, dtype=F32)
    log_g = jnp.log1p(-jnp.exp2(-(5.0 + 2.0 * hh)))
    idx = jnp.arange(C, dtype=F32)
    dm = jnp.exp(jnp.abs(idx[:, None] - idx[None, :])[None] * log_g[:, None, None])

    def lanes(e):
        return jnp.broadcast_to(jnp.exp(e[None, :] * log_g[:, None])[:, :, None], (RET_HEADS, e.shape[0], dh))

    rt = jnp.stack([lanes(idx + 1.0), lanes(C - idx), lanes(C - 1.0 - idx), lanes(idx)], axis=1)
    pos = jnp.arange(c_len, dtype=F32)
    cwt = jnp.stack([lanes(c_len - 1.0 - pos), lanes(pos)], axis=1)
    gc = jnp.broadcast_to(jnp.exp(C * log_g)[:, None, None], (RET_HEADS, SUBLANES, dh))
    rows = t // GRID_W
    row = jnp.repeat(jnp.arange(rows, dtype=F32), GRID_W)
    col = jnp.tile(jnp.arange(GRID_W, dtype=F32), rows)
    q4 = dh // 4
    freqs = ROPE_BASE ** (-jnp.arange(q4, dtype=F32) / q4)
    ang_r = row[:, None] * freqs[None, :]
    ang_c = col[:, None] * freqs[None, :]
    cos = jnp.concatenate([jnp.cos(ang_r), jnp.cos(ang_c)] * 2, axis=1)
    sin = jnp.concatenate([-jnp.sin(ang_r), -jnp.sin(ang_c), jnp.sin(ang_r), jnp.sin(ang_c)], axis=1)
    return dm, rt, cwt, gc, cos, sin


def _pair_apart(w_in, n_xg):
    d, n_all = w_in.shape
    dh = (n_all - n_xg) // (4 * RET_HEADS)
    qk = w_in[:, n_xg:n_xg + 2 * RET_HEADS * dh].reshape(d, 2 * RET_HEADS, 2, 2, dh // 4)
    qk = jnp.swapaxes(qk, 2, 3).reshape(d, 2 * RET_HEADS * dh)
    return jnp.concatenate([w_in[:, :n_xg], qk, w_in[:, n_xg + 2 * RET_HEADS * dh:]], axis=1)


def _ret_call(qkvg, qkvg_c):
    b, t, w = qkvg.shape
    c_len = qkvg_c.shape[1]
    dh = w // (4 * RET_HEADS)
    nc = t // RET_CHUNK
    dm, rt, cwt, gc, cos, sin = _ret_tables(t, c_len, dh)
    H = RET_HEADS

    def col(off):
        return lambda bi, h: (bi, 0, off + h)

    return pl.pallas_call(
        functools.partial(_ret_kernel, nc=nc, scale=dh ** -0.5),
        grid=(b, H),
        in_specs=[
            pl.BlockSpec((1, t, dh), col(0)),
            pl.BlockSpec((1, t, dh), col(H)),
            pl.BlockSpec((1, t, dh), col(2 * H)),
            pl.BlockSpec((1, t, dh), col(3 * H)),
            pl.BlockSpec((1, c_len, dh), col(H)),
            pl.BlockSpec((1, c_len, dh), col(2 * H)),
            pl.BlockSpec((t, dh), lambda bi, h: (0, 0)),
            pl.BlockSpec((t, dh), lambda bi, h: (0, 0)),
            pl.BlockSpec((1, RET_CHUNK, RET_CHUNK), lambda bi, h: (h, 0, 0)),
            pl.BlockSpec((1, 4, RET_CHUNK, dh), lambda bi, h: (h, 0, 0, 0)),
            pl.BlockSpec((1, 2, c_len, dh), lambda bi, h: (h, 0, 0, 0)),
            pl.BlockSpec((1, SUBLANES, dh), lambda bi, h: (h, 0, 0)),
        ],
        out_specs=pl.BlockSpec((1, t, dh), col(0)),
        out_shape=jax.ShapeDtypeStruct((b, t, H * dh), BF16),
        scratch_shapes=[
            pltpu.VMEM((t, dh), F32),
            pltpu.VMEM((t, dh), BF16),
            pltpu.VMEM((nc, dh, dh), F32),
            pltpu.VMEM((nc, dh, dh), F32),
        ],
        compiler_params=_cparams("parallel", "arbitrary"),
        name="retention",
    )(qkvg, qkvg, qkvg, qkvg, qkvg_c, qkvg_c, cos, sin, dm, rt, cwt, gc)


def _route(logits_t, bias):
    e, n = logits_t.shape
    per = e // N_GROUPS
    s = jax.nn.sigmoid(logits_t)
    sb = s + bias
    neg = -jnp.inf
    g3 = sb.reshape(N_GROUPS, per, n)
    it3 = lax.broadcasted_iota(jnp.int32, g3.shape, 1)
    m1 = jnp.max(g3, axis=1, keepdims=True)
    i1 = jnp.min(jnp.where(g3 == m1, it3, per), axis=1, keepdims=True)
    m2 = jnp.max(jnp.where(it3 == i1, neg, g3), axis=1, keepdims=True)
    gs = (m1 + m2).reshape(N_GROUPS, n)
    itg = lax.broadcasted_iota(jnp.int32, gs.shape, 0)
    gsel = jnp.zeros(gs.shape, jnp.bool_)
    for _ in range(TOP_GROUPS):
        m = jnp.max(gs, axis=0, keepdims=True)
        i = jnp.min(jnp.where(gs == m, itg, N_GROUPS), axis=0, keepdims=True)
        hit = itg == i
        gsel = gsel | hit
        gs = jnp.where(hit, neg, gs)
    gmask = jnp.broadcast_to(gsel.reshape(N_GROUPS, 1, n), (N_GROUPS, per, n)).reshape(e, n)
    v = jnp.where(gmask, sb, neg)
    ite = lax.broadcasted_iota(jnp.int32, v.shape, 0)
    sel = jnp.zeros(v.shape, jnp.bool_)
    for _ in range(TOP_K):
        m = jnp.max(v, axis=0, keepdims=True)
        i = jnp.min(jnp.where(v == m, ite, e), axis=0, keepdims=True)
        hit = ite == i
        sel = sel | hit
        v = jnp.where(hit, neg, v)
    picked = jnp.where(sel, s, 0.0)
    return picked / jnp.sum(picked, axis=0, keepdims=True) * ROUTE_SCALE


def _outproj_kernel(yl_ref, yr_ref, x_ref, g1_ref, sh_ref, sc_ref, nf_ref, wo_ref, wr_ref, rb_ref,
                    x1_ref, h2_ref, gt_ref, cnt_ref):
    dl = yl_ref.shape[2]
    ts = gt_ref.shape[1] // cnt_ref.shape[0]
    w = wr_ref[...]
    w_hi = w.astype(BF16)
    r1 = w - w_hi.astype(F32)
    w_mid = r1.astype(BF16)
    w_lo = (r1 - w_mid.astype(F32)).astype(BF16)
    for k in range(cnt_ref.shape[0]):
        rows = pl.ds(k * ts, ts)
        mx = jnp.dot(yl_ref[0, rows, :], wo_ref[0:dl, :], preferred_element_type=F32)
        mx = mx + jnp.dot(yr_ref[0, rows, :], wo_ref[dl:, :], preferred_element_type=F32)
        x1 = x_ref[0, rows, :] + g1_ref[0] * mx
        x1_ref[0, rows, :] = x1
        h = _rms_mod(x1, nf_ref[...], sh_ref[0], sc_ref[0])
        h_hi = h.astype(BF16)
        h2_ref[0, rows, :] = h_hi
        h_lo = (h - h_hi.astype(F32)).astype(BF16)
        small = (_dot32(h_hi, w_lo) + _dot32(h_lo, w_mid)) + (_dot32(h_hi, w_mid) + _dot32(h_lo, w_hi))
        logits = small + _dot32(h_hi, w_hi)
        ne = logits.shape[1]
        logits_t = jnp.concatenate([logits, jnp.zeros((ts, LANES - ne), F32)], axis=1).T[:ne, :]
        gates = _route(logits_t, rb_ref[...])
        gt_ref[:, rows] = gates
        hit = (gates > 0.0).astype(BF16)
        cnt_ref[k] = lax.dot_general(jnp.ones((SUBLANES, ts), BF16), hit, NT_DIMS, preferred_element_type=F32)


def _outproj_call(y_lru, y_ret, x, g1, sh2, sc2, norm_ffn, w_out_bf, w_router, b_router):
    b, t, d = x.shape
    dl = y_lru.shape[2]
    e = w_router.shape[1]
    ts = min(t, MOE_TILE)
    tt = min(t, OUTPROJ_TILES * ts)
    nt = t // tt
    per = tt // ts
    mod_map = lambda bi, i: (bi, 0, 0)
    return pl.pallas_call(
        _outproj_kernel,
        grid=(b, nt),
        in_specs=[
            pl.BlockSpec((1, tt, dl), lambda bi, i: (bi, i, 0)),
            pl.BlockSpec((1, tt, d - dl), lambda bi, i: (bi, i, 0)),
            pl.BlockSpec((1, tt, d), lambda bi, i: (bi, i, 0)),
            pl.BlockSpec((1, 1, d), mod_map),
            pl.BlockSpec((1, 1, d), mod_map),
            pl.BlockSpec((1, 1, d), mod_map),
            pl.BlockSpec((1, d), lambda bi, i: (0, 0)),
            pl.BlockSpec((d, d), lambda bi, i: (0, 0)),
            pl.BlockSpec((d, e), lambda bi, i: (0, 0)),
            pl.BlockSpec((e, 1), lambda bi, i: (0, 0)),
        ],
        out_specs=[
            pl.BlockSpec((1, tt, d), lambda bi, i: (bi, i, 0)),
            pl.BlockSpec((1, tt, d), lambda bi, i: (bi, i, 0)),
            pl.BlockSpec((e, tt), lambda bi, i: (0, bi * nt + i)),
            pl.BlockSpec((per, SUBLANES, e), lambda bi, i: (bi * nt + i, 0, 0)),
        ],
        out_shape=[
            jax.ShapeDtypeStruct((b, t, d), F32),
            jax.ShapeDtypeStruct((b, t, d), BF16),
            jax.ShapeDtypeStruct((e, b * t), F32),
            jax.ShapeDtypeStruct((b * nt * per, SUBLANES, e), F32),
        ],
        compiler_params=_cparams("parallel", "arbitrary"),
        name="outproj_route",
    )(y_lru, y_ret, x, g1, sh2, sc2, norm_ffn.reshape(1, d), w_out_bf, w_router, b_router.reshape(e, 1))


MOE_TILE = 256
MOE_CHUNK = 16
MOE_RB = 512
MOE_ROWS_COMMON = 2048
MOE_BM = 1024
OUTPROJ_TILES = 2
NO_RANK = 512.0

TN_DIMS = (((0,), (0,)), ((), ()))
TAB_LOFF, TAB_GBASE, TAB_CPAD, TAB_NROWS, TAB_EEND = range(5)


def _round_up(v, m):
    return -(-v // m) * m


def _tables_kernel(cnt_ref, loff_ref, gbase_ref, cpad_ref, eend_ref, blk_ref, nused_ref, ctab_ref, *, nb):
    cnt = cnt_ref[...]
    ns, ne = cnt.shape
    cpad = jnp.ceil(cnt * (1.0 / MOE_CHUNK)) * MOE_CHUNK
    upper = (lax.broadcasted_iota(jnp.int32, (ne, ne), 0) < lax.broadcasted_iota(jnp.int32, (ne, ne), 1)).astype(F32)
    lower = (lax.broadcasted_iota(jnp.int32, (ns, ns), 1) < lax.broadcasted_iota(jnp.int32, (ns, ns), 0)).astype(F32)
    loff = jnp.dot(cpad, upper, precision=HIGHEST, preferred_element_type=F32)
    before = jnp.dot(lower, cpad, precision=HIGHEST, preferred_element_type=F32)
    tot = jnp.sum(cpad, axis=0, keepdims=True)
    epad = jnp.ceil(tot * (1.0 / MOE_BM)) * MOE_BM
    epad8 = jnp.broadcast_to(epad, (SUBLANES, ne))
    ebase = jnp.dot(epad8, upper, precision=HIGHEST, preferred_element_type=F32)[0:1]
    eend = ebase + epad
    loff_ref[...] = loff
    gbase_ref[...] = ebase + before
    cpad_ref[...] = cpad
    eend_ref[...] = eend
    start = lax.broadcasted_iota(jnp.int32, (nb, ne), 0).astype(F32) * MOE_BM
    blk = jnp.sum((eend <= start).astype(F32), axis=1, keepdims=True)
    blk_ref[...] = jnp.minimum(blk, ne - 1.0)
    nused_ref[...] = jnp.sum(epad, axis=1, keepdims=True) * (1.0 / MOE_BM)
    nch = ctab_ref.shape[1]
    cidx = lax.broadcasted_iota(jnp.int32, (ns, nch), 1).astype(F32)
    shift = (ebase + before - loff) * (1.0 / MOE_CHUNK)
    ctab = jnp.full((ns, nch), -1.0, F32)
    for e in range(ne):
        lo = loff[:, e:e + 1] * (1.0 / MOE_CHUNK)
        hi = lo + cpad[:, e:e + 1] * (1.0 / MOE_CHUNK)
        ctab = jnp.where((cidx >= lo) & (cidx < hi), cidx + shift[:, e:e + 1], ctab)
    ctab_ref[...] = ctab


def _tables_call(cnt, nb, nch):
    ns, ne = cnt.shape
    f = lambda shape: jax.ShapeDtypeStruct(shape, F32)
    return pl.pallas_call(
        functools.partial(_tables_kernel, nb=nb),
        out_shape=[f((ns, ne)), f((ns, ne)), f((ns, ne)), f((1, ne)), f((nb, 1)), f((1, 1)), f((ns, nch))],
        compiler_params=pltpu.CompilerParams(vmem_limit_bytes=VMEM_LIMIT_BYTES),
        name="moe_tables",
    )(cnt)


def _tile_rank_rhs(gt, lcol):
    ne, ts = gt.shape
    hit = gt > 0.0
    before = (lax.broadcasted_iota(jnp.int32, (ts, ts), 0) < lax.broadcasted_iota(jnp.int32, (ts, ts), 1)).astype(BF16)
    rank = jnp.dot(hit.astype(BF16), before, preferred_element_type=F32)
    rm = jnp.where(hit, rank, NO_RANK)
    q = jnp.broadcast_to(lcol * (1.0 / MOE_CHUNK), (ne, LANES))
    return jnp.concatenate([rm, q], axis=1).astype(BF16)


def _onehot_rows(r0, rhs, lrow, crow, ts):
    ne = lrow.shape[1]
    rows = (r0 + lax.broadcasted_iota(jnp.int32, (MOE_RB, ne), 0)).astype(F32)
    seg = ((rows >= lrow) & (rows < crow)).astype(BF16)
    ex = jnp.dot(seg, rhs, preferred_element_type=F32)
    rowl = (r0 + lax.broadcasted_iota(jnp.int32, (MOE_RB, LANES), 0)).astype(F32)
    tgt = rowl - MOE_CHUNK * ex[:, ts:ts + LANES]
    p = ex[:, :ts] == jnp.concatenate([tgt] * (ts // LANES), axis=1)
    return p, (ex[:, ts + LANES:] if rhs.shape[1] > ts + LANES else None)


def _chunk_copy(buf, slot, c, row, hbm, sem, to_hbm):
    loc = buf.at[slot, pl.ds(c * MOE_CHUNK, MOE_CHUNK)]
    glob = hbm.at[pl.ds(pl.multiple_of(row, MOE_CHUNK), MOE_CHUNK)]
    return pltpu.make_async_copy(loc, glob, sem.at[slot]) if to_hbm else pltpu.make_async_copy(glob, loc, sem.at[slot])


def _wait_chunks(n, buf, slot, hbm, sem, to_hbm):
    for _ in range(n):
        _chunk_copy(buf, slot, 0, 0, hbm, sem, to_hbm).wait()


def _wait_chunks_dyn(n, buf, slot, hbm, sem, to_hbm):
    def body(i, c):
        _chunk_copy(buf, slot, 0, 0, hbm, sem, to_hbm).wait()
        return c

    lax.fori_loop(0, n, body, 0)


def _dispatch_kernel(tab_ref, ctab_ref, h_ref, gt_ref, lrow_ref, crow_ref, lcol_ref, xs_hbm, xbuf, zbuf, sem, tailf,
                     *, ns, trash0):
    s = pl.program_id(0)
    slot = s % 2
    ts = h_ref.shape[0]
    ne = gt_ref.shape[0]
    rows_max = xbuf.shape[1]
    nch = rows_max // MOE_CHUNK
    rows_c = min(rows_max, MOE_ROWS_COMMON)
    nch_c = rows_c // MOE_CHUNK

    def wait_slot(sl):
        _wait_chunks(nch_c, xbuf, sl, xs_hbm, sem, True)
        if nch > nch_c:
            @pl.when(tailf[sl] == 1)
            def _():
                _wait_chunks(nch - nch_c, xbuf, sl, xs_hbm, sem, True)

    @pl.when(s >= 2)
    def _():
        wait_slot(slot)

    rhs = _tile_rank_rhs(gt_ref[...], lcol_ref[0])

    def rows_block(r0):
        p, _ = _onehot_rows(r0, rhs, lrow_ref[0], crow_ref[0], ts)
        xbuf[slot, r0:r0 + MOE_RB, :] = jnp.dot(p.astype(BF16), h_ref[...],
                                                preferred_element_type=F32).astype(xbuf.dtype)

    def copy_chunks(c0, c1):
        for c in range(c0, c1):
            g = ctab_ref[0, 0, c]
            row = jnp.where(g >= 0, g * MOE_CHUNK, trash0 + slot * rows_max + c * MOE_CHUNK)
            _chunk_copy(xbuf, slot, c, row, xs_hbm, sem, True).start()

    for r0 in range(0, rows_c, MOE_RB):
        rows_block(r0)
    copy_chunks(0, nch_c)
    if nch > nch_c:
        long_tile = ctab_ref[0, 0, nch_c] >= 0
        tailf[slot] = long_tile.astype(jnp.int32)

        @pl.when(long_tile)
        def _():
            for r0 in range(rows_c, rows_max, MOE_RB):
                rows_block(r0)
            copy_chunks(nch_c, nch)

    @pl.when(s == ns - 1)
    def _():
        zbuf[...] = jnp.zeros(zbuf.shape, zbuf.dtype)

        def per_expert(e, total):
            first = tab_ref[0, TAB_GBASE, e] + tab_ref[0, TAB_CPAD, e]
            ntail = lax.shift_right_logical(tab_ref[0, TAB_EEND, e] - first, MOE_CHUNK.bit_length() - 1)

            def per_chunk(j, c2):
                dst = xs_hbm.at[pl.ds(pl.multiple_of(first + j * MOE_CHUNK, MOE_CHUNK), MOE_CHUNK)]
                pltpu.make_async_copy(zbuf, dst, sem.at[slot]).start()
                return c2

            lax.fori_loop(0, ntail, per_chunk, 0)
            return total + ntail

        ntail = lax.fori_loop(0, ne, per_expert, 0)
        _wait_chunks_dyn(ntail, xbuf, slot, xs_hbm, sem, True)
        wait_slot(slot)
        if ns >= 2:
            wait_slot(1 - slot)


def _dispatch_call(tab, ctab, h2, gt, lrow, crow, lcol, p_max, rows_max):
    n, d = h2.shape
    ne = gt.shape[0]
    ts = min(n, MOE_TILE)
    ns = n // ts
    nch = rows_max // MOE_CHUNK
    return pl.pallas_call(
        functools.partial(_dispatch_kernel, ns=ns, trash0=p_max),
        grid=(ns,),
        in_specs=[
            pl.BlockSpec((1, 5, ne), lambda s: (s, 0, 0), memory_space=pltpu.SMEM),
            pl.BlockSpec((1, 1, nch), lambda s: (s, 0, 0), memory_space=pltpu.SMEM),
            pl.BlockSpec((ts, d), lambda s: (s, 0)),
            pl.BlockSpec((ne, ts), lambda s: (0, s)),
            pl.BlockSpec((1, 1, ne), lambda s: (s, 0, 0)),
            pl.BlockSpec((1, 1, ne), lambda s: (s, 0, 0)),
            pl.BlockSpec((1, ne, 1), lambda s: (s, 0, 0)),
        ],
        out_specs=pl.BlockSpec(memory_space=pl.ANY),
        out_shape=jax.ShapeDtypeStruct((p_max + 2 * rows_max, d), BF16),
        scratch_shapes=[
            pltpu.VMEM((2, rows_max, d), BF16),
            pltpu.VMEM((MOE_CHUNK, d), BF16),
            pltpu.SemaphoreType.DMA((2,)),
            pltpu.SMEM((2,), jnp.int32),
        ],
        compiler_params=_cparams("arbitrary"),
        name="moe_dispatch",
    )(tab, ctab, h2, gt, lrow, crow, lcol)


def _experts_kernel(blk_ref, nused_ref, x_ref, w1_ref, w3_ref, w2_ref, y_ref):
    del blk_ref

    @pl.when(pl.program_id(0) < nused_ref[0])
    def _():
        xv = x_ref[...]
        hid = _silu(jnp.dot(xv, w1_ref[0], preferred_element_type=F32)) * jnp.dot(xv, w3_ref[0], preferred_element_type=F32)
        y_ref[...] = jnp.dot(hid.astype(BF16), w2_ref[0], preferred_element_type=F32).astype(y_ref.dtype)


def _experts_call(blk, nused, xs, w1, w3, w2):
    d = xs.shape[1]
    _, _, de = w1.shape
    nb = blk.shape[0]
    p_max = nb * MOE_BM
    row_map = lambda i, blk_r, nu_r: (jnp.maximum(jnp.minimum(i, nu_r[0] - 1), 0), 0)
    w_map = lambda i, blk_r, nu_r: (blk_r[i], 0, 0)
    return pl.pallas_call(
        _experts_kernel,
        grid_spec=pltpu.PrefetchScalarGridSpec(
            num_scalar_prefetch=2,
            grid=(nb,),
            in_specs=[
                pl.BlockSpec((MOE_BM, d), row_map),
                pl.BlockSpec((1, d, de), w_map),
                pl.BlockSpec((1, d, de), w_map),
                pl.BlockSpec((1, de, d), w_map),
            ],
            out_specs=pl.BlockSpec((MOE_BM, d), row_map),
        ),
        out_shape=jax.ShapeDtypeStruct((p_max, d), BF16),
        compiler_params=_cparams("arbitrary"),
        name="moe_experts",
    )(blk, nused, xs, w1, w3, w2)


def _combine_kernel(ctab_ref, ctabn_ref, gt_ref, lrow_ref, crow_ref, lcol_ref, h_ref, x1_ref, g2_ref,
                    s1_ref, s3_ref, s2_ref, nfin_ref, ys_hbm, o_ref, ybuf, acc_ref, sem, *, ns):
    s = pl.program_id(0)
    slot = s % 2
    ts = h_ref.shape[0]
    rows_max = ybuf.shape[1]
    nch = rows_max // MOE_CHUNK
    rows_c = min(rows_max, MOE_ROWS_COMMON)
    nch_c = rows_c // MOE_CHUNK

    def chunk_range(tab, sl, c0, c1):
        for c in range(c0, c1):
            row = jnp.maximum(tab[0, 0, c], 0) * MOE_CHUNK
            _chunk_copy(ybuf, sl, c, row, ys_hbm, sem, False).start()

    def gather(tab, sl):
        chunk_range(tab, sl, 0, nch_c)
        if nch > nch_c:
            @pl.when(tab[0, 0, nch_c] >= 0)
            def _():
                chunk_range(tab, sl, nch_c, nch)

    def wait(tab, sl):
        _wait_chunks(nch_c, ybuf, sl, ys_hbm, sem, False)
        if nch > nch_c:
            @pl.when(tab[0, 0, nch_c] >= 0)
            def _():
                _wait_chunks(nch - nch_c, ybuf, sl, ys_hbm, sem, False)

    @pl.when(s == 0)
    def _():
        gather(ctab_ref, 0)

    gather(ctabn_ref, 1 - slot)

    h = h_ref[...]
    hs = _silu(jnp.dot(h, s1_ref[...], preferred_element_type=F32)) * jnp.dot(h, s3_ref[...], preferred_element_type=F32)
    acc = jnp.dot(hs.astype(BF16), s2_ref[...], preferred_element_type=F32)

    gt = gt_ref[...]
    rhs = jnp.concatenate([_tile_rank_rhs(gt, lcol_ref[0]), gt.astype(BF16)], axis=1)
    wait(ctab_ref, slot)

    def rows_block(r0):
        p, gexp = _onehot_rows(r0, rhs, lrow_ref[0], crow_ref[0], ts)
        gm = jnp.where(p, gexp, 0.0).astype(BF16)
        return lax.dot_general(gm, ybuf[slot, r0:r0 + MOE_RB, :], TN_DIMS, preferred_element_type=F32)

    for r0 in range(0, rows_c, MOE_RB):
        acc = acc + rows_block(r0)
    if nch > nch_c:
        acc_ref[...] = acc

        @pl.when(ctab_ref[0, 0, nch_c] >= 0)
        def _():
            for r0 in range(rows_c, rows_max, MOE_RB):
                acc_ref[...] += rows_block(r0)

        acc = acc_ref[...]
    xo = x1_ref[...] + g2_ref[0] * acc
    o_ref[...] = xo * lax.rsqrt(jnp.mean(xo * xo, axis=-1, keepdims=True) + EPS) * nfin_ref[...]

    @pl.when(s == ns - 1)
    def _():
        wait(ctabn_ref, 1 - slot)


def _combine_call(ctab, gt, lrow, crow, lcol, h2, x1, g2, ws1, ws3, ws2, norm_final, ys, rows_max, t):
    n, d = h2.shape
    ne = gt.shape[0]
    ts = min(n, MOE_TILE)
    ns = n // ts
    per_b = t // ts
    nch = rows_max // MOE_CHUNK
    return pl.pallas_call(
        functools.partial(_combine_kernel, ns=ns),
        grid=(ns,),
        in_specs=[
            pl.BlockSpec((1, 1, nch), lambda s: (s, 0, 0), memory_space=pltpu.SMEM),
            pl.BlockSpec((1, 1, nch), lambda s: (jnp.minimum(s + 1, ns - 1), 0, 0), memory_space=pltpu.SMEM),
            pl.BlockSpec((ne, ts), lambda s: (0, s)),
            pl.BlockSpec((1, 1, ne), lambda s: (s, 0, 0)),
            pl.BlockSpec((1, 1, ne), lambda s: (s, 0, 0)),
            pl.BlockSpec((1, ne, 1), lambda s: (s, 0, 0)),
            pl.BlockSpec((ts, d), lambda s: (s, 0)),
            pl.BlockSpec((ts, d), lambda s: (s, 0)),
            pl.BlockSpec((1, 1, d), lambda s: (s // per_b, 0, 0)),
            pl.BlockSpec(ws1.shape, lambda s: (0, 0)),
            pl.BlockSpec(ws3.shape, lambda s: (0, 0)),
            pl.BlockSpec(ws2.shape, lambda s: (0, 0)),
            pl.BlockSpec((1, d), lambda s: (0, 0)),
            pl.BlockSpec(memory_space=pl.ANY),
        ],
        out_specs=pl.BlockSpec((ts, d), lambda s: (s, 0)),
        out_shape=jax.ShapeDtypeStruct((n, d), F32),
        scratch_shapes=[
            pltpu.VMEM((2, rows_max, d), BF16),
            pltpu.VMEM((ts, d), F32),
            pltpu.SemaphoreType.DMA((2,)),
        ],
        compiler_params=_cparams("arbitrary"),
        name="moe_combine",
    )(ctab, ctab, gt, lrow, crow, lcol, h2, x1, g2, ws1, ws3, ws2, norm_final.reshape(1, d), ys)


def _moe_call(h2, gt, cnt, x1, g2, w1, w3, w2, ws1, ws3, ws2, norm_final, t):
    n, d = h2.shape
    ne = gt.shape[0]
    ts = min(n, MOE_TILE)
    ns = n // ts
    rows_max = _round_up(TOP_K * ts + ne * (MOE_CHUNK - 1), MOE_RB)
    p_max = _round_up(n * TOP_K + ns * ne * (MOE_CHUNK - 1) + ne * (MOE_BM - 1), MOE_BM)
    nb = p_max // MOE_BM

    nch = rows_max // MOE_CHUNK
    loff, gbase, cpad, eend, blk, nused, ctab = _tables_call(cnt[:, 0, :], nb, nch)
    nrows = loff[:, ne - 1:] + cpad[:, ne - 1:]
    tab = jnp.stack([loff, gbase, cpad, jnp.broadcast_to(nrows, (ns, ne)), jnp.broadcast_to(eend, (ns, ne))],
                    axis=1).astype(jnp.int32)
    ctab = ctab.astype(jnp.int32).reshape(ns, 1, nch)
    lrow = loff.reshape(ns, 1, ne)
    crow = (loff + cpad).reshape(ns, 1, ne)
    lcol = loff.reshape(ns, ne, 1)

    xs = _dispatch_call(tab, ctab, h2, gt, lrow, crow, lcol, p_max, rows_max)
    ys = _experts_call(blk.reshape(nb).astype(jnp.int32), nused.reshape(1).astype(jnp.int32), xs, w1, w3, w2)
    return _combine_call(ctab, gt, lrow, crow, lcol, h2, x1, g2, ws1, ws3, ws2, norm_final, ys, rows_max, t)


def _block_diag(w):
    nb, n, _ = w.shape
    eye = jnp.eye(nb, dtype=w.dtype)
    return (eye[:, None, :, None] * w[:, :, None, :]).reshape(nb * n, nb * n)


def _lru_gate_weights(w_rec, b_rec, w_inp, b_inp):
    d_lru = b_rec.shape[1]
    half = d_lru // 2
    ws, bs = [], []
    for dirn in range(2):
        wr, wi = _block_diag(w_rec[dirn]), _block_diag(w_inp[dirn])
        ws.append(jnp.stack([jnp.concatenate([wr[s:s + half, s:s + half], wi[s:s + half, s:s + half]], axis=1)
                             for s in (0, half)]))
        bs.append(jnp.stack([jnp.concatenate([b_rec[dirn, s:s + half], b_inp[dirn, s:s + half]])[None, :]
                             for s in (0, half)]))
    return jnp.stack(ws).astype(BF16), jnp.stack(bs).astype(F32)


def kernel(x, c, ctx, c_ctx, w_mod, b_mod, norm_mix, norm_ffn, w_in, conv_w, conv_b, lru_w_rec, lru_b_rec, lru_w_in, lru_b_in, lru_lambda, w_out, w_router, b_router, w_exp_gate, w_exp_up, w_exp_down, w_sh_gate, w_sh_up, w_sh_down, norm_final):
    b, t, d = x.shape
    depth = w_mod.shape[0]
    assert depth == 1, "context-stream update for deeper stacks is not implemented"
    l = 0
    d_lru = conv_w.shape[2]
    n_xg = 2 * d_lru

    rows = -(-(b + 1) // SUBLANES) * SUBLANES
    cc = jnp.concatenate([c, c_ctx[None, :], jnp.zeros((rows - b - 1, d), F32)], axis=0)
    mod = _mod_call(cc, w_mod[l], b_mod[l]).reshape(rows, 1, -1)
    sh1, sc1, g1, sh2, sc2, g2 = (mod[:, :, i * d:(i + 1) * d] for i in range(6))

    w_in_bf = _pair_apart(w_in[l], n_xg).astype(BF16)
    gain_mix = norm_mix[l].reshape(1, d)
    xg, qkvg = _inproj_call(x, sh1[:b], sc1[:b], gain_mix, w_in_bf, n_xg, True)
    xg_c, qkvg_c = _inproj_call(ctx, sh1[b:b + 1], sc1[b:b + 1], gain_mix, w_in_bf, n_xg, False)

    wg, bg = _lru_gate_weights(lru_w_rec[l], lru_b_rec[l], lru_w_in[l], lru_b_in[l])
    y_lru = _lru_call(xg, xg_c, conv_w[l], conv_b[l], wg, bg, lru_lambda[l])
    y_ret = _ret_call(qkvg, qkvg_c)

    x1, h2, gt, cnt = _outproj_call(y_lru, y_ret, x, g1[:b], sh2[:b], sc2[:b], norm_ffn[l], w_out[l].astype(BF16),
                                    w_router[l], b_router[l])

    out = _moe_call(h2.reshape(b * t, d), gt, cnt, x1.reshape(b * t, d), g2[:b],
                    w_exp_gate[l].astype(BF16), w_exp_up[l].astype(BF16), w_exp_down[l].astype(BF16),
                    w_sh_gate[l].astype(BF16), w_sh_up[l].astype(BF16), w_sh_down[l].astype(BF16), norm_final, t)
    return out.reshape(b, t, d)
```

```python
import functools

import jax
import jax.numpy as jnp
from jax import lax
from jax.experimental import pallas as pl
from jax.experimental.pallas import tpu as pltpu

F32 = jnp.float32
BF16 = jnp.bfloat16
HIGHEST = lax.Precision.HIGHEST

EPS = 1e-6
GRID_W = 64
LRU_C = 8.0
RET_HEADS = 4
RET_CHUNK = 128
RET_UNROLL = 32
ROPE_BASE = 10000.0
N_GROUPS = 8
TOP_GROUPS = 4
TOP_K = 6
ROUTE_SCALE = 2.5

LANES = 128
SUBLANES = 8
VMEM_LIMIT_BYTES = 56 * 1024 * 1024

NT_DIMS = (((1,), (1,)), ((), ()))


def _cparams(*sem):
    return pltpu.CompilerParams(dimension_semantics=sem, vmem_limit_bytes=VMEM_LIMIT_BYTES)


def _silu(v):
    return v * jax.nn.sigmoid(v)


def _dot32(a, b):
    return jnp.dot(a, b, preferred_element_type=F32)


def _rms_mod(x, gain, shift, scale):
    y = x * lax.rsqrt(jnp.mean(x * x, axis=-1, keepdims=True) + EPS) * gain
    return y * (1.0 + scale) + shift


def _mod_kernel(c_ref, w_ref, b_ref, o_ref):
    s = _silu(c_ref[...])
    o_ref[...] = jnp.dot(s, w_ref[...], precision=HIGHEST, preferred_element_type=F32) + b_ref[...]


def _mod_call(cc, w_mod, b_mod):
    rows, d = cc.shape
    n = w_mod.shape[1]
    tn = 512
    return pl.pallas_call(
        _mod_kernel,
        grid=(n // tn,),
        in_specs=[
            pl.BlockSpec((rows, d), lambda j: (0, 0)),
            pl.BlockSpec((d, tn), lambda j: (0, j)),
            pl.BlockSpec((1, tn), lambda j: (0, j)),
        ],
        out_specs=pl.BlockSpec((rows, tn), lambda j: (0, j)),
        out_shape=jax.ShapeDtypeStruct((rows, n), F32),
        compiler_params=_cparams("arbitrary"),
        name="mod",
    )(cc, w_mod, b_mod.reshape(1, n))


def _inproj_kernel(x_ref, sh_ref, sc_ref, g_ref, w_ref, xg_ref, qkvg_ref, *, n_xg):
    h = _rms_mod(x_ref[0], g_ref[...], sh_ref[0], sc_ref[0]).astype(BF16)
    xg_ref[0] = jnp.dot(h, w_ref[:, :n_xg], preferred_element_type=F32)
    qkvg_ref[0] = jnp.dot(h, w_ref[:, n_xg:], preferred_element_type=F32)


def _inproj_call(x, shift, scale, gain, w_in_bf, n_xg, per_batch_mod):
    b, t, d = x.shape
    n_all = w_in_bf.shape[1]
    tt = min(t, 512)
    mod_map = (lambda bi, i: (bi, 0, 0)) if per_batch_mod else (lambda bi, i: (0, 0, 0))
    return pl.pallas_call(
        functools.partial(_inproj_kernel, n_xg=n_xg),
        grid=(b, t // tt),
        in_specs=[
            pl.BlockSpec((1, tt, d), lambda bi, i: (bi, i, 0)),
            pl.BlockSpec((1, 1, d), mod_map),
            pl.BlockSpec((1, 1, d), mod_map),
            pl.BlockSpec((1, d), lambda bi, i: (0, 0)),
            pl.BlockSpec((d, n_all), lambda bi, i: (0, 0)),
        ],
        out_specs=[
            pl.BlockSpec((1, tt, n_xg), lambda bi, i: (bi, i, 0)),
            pl.BlockSpec((1, tt, n_all - n_xg), lambda bi, i: (bi, i, 0)),
        ],
        out_shape=[
            jax.ShapeDtypeStruct((b, t, n_xg), F32),
            jax.ShapeDtypeStruct((b, t, n_all - n_xg), F32),
        ],
        compiler_params=_cparams("parallel", "arbitrary"),
        name="inproj",
    )(x, shift, scale, gain, w_in_bf)


SCAN_ROWS = 64


def _tile_scan(a, b, reverse):
    row = lax.broadcasted_iota(jnp.int32, a.shape, 0)
    for s in (1, 2, 4):
        if reverse:
            a_s = pltpu.roll(a, SUBLANES - s, 0)
            b_s = pltpu.roll(b, SUBLANES - s, 0)
            m = row < SUBLANES - s
        else:
            a_s = pltpu.roll(a, s, 0)
            b_s = pltpu.roll(b, s, 0)
            m = row >= s
        b = jnp.where(m, a * b_s + b, b)
        a = jnp.where(m, a * a_s, a)
    return a, b


def _lru_kernel(xa_ref, ga_ref, xac_ref, cw_ref, cb_ref, wg_ref, bg_ref, lam_ref, y_ref,
                hf_ref, u_ref, xe_ref, a_ref, b_ref, carry_ref, *, t_len, c_len, tc, nc):
    j = pl.program_id(1)
    half = wg_ref.shape[2]

    def conv(n):
        u = cb_ref[...]
        for k in range(cw_ref.shape[0]):
            u = u + cw_ref[k:k + 1, :] * xe_ref[6 + k:6 + k + n, :]
        return u

    def gates(u, n, dirn):
        lam = -lam_ref[dirn:dirn + 1, :]
        sp = jnp.maximum(lam, 0.0) + jnp.log1p(jnp.exp(-jnp.abs(lam)))
        for hf in range(2):
            uh = u[:, hf * half:(hf + 1) * half]
            z = jnp.dot(uh.astype(BF16), wg_ref[dirn, hf], preferred_element_type=F32) + bg_ref[dirn, hf]
            r = jax.nn.sigmoid(z[:, :half])
            i = jax.nn.sigmoid(z[:, half:])
            log_a = -LRU_C * r * sp[:, hf * half:(hf + 1) * half]
            a = jnp.exp(log_a)
            a_ref[0:n, hf * half:(hf + 1) * half] = a
            b_ref[0:n, hf * half:(hf + 1) * half] = jnp.sqrt((1.0 - a) * (1.0 + a)) * (i * uh)

    def scan(n, dirn, emit):
        ng = n // SCAN_ROWS
        nt = SCAN_ROWS // SUBLANES

        def body(gi, carry):
            g = gi if dirn == 0 else ng - 1 - gi
            base = pl.multiple_of(g * SCAN_ROWS, SCAN_ROWS)
            av = a_ref[pl.ds(base, SCAN_ROWS), :]
            bv = b_ref[pl.ds(base, SCAN_ROWS), :]
            hs = [None] * nt
            order = range(nt) if dirn == 0 else range(nt - 1, -1, -1)
            for k in order:
                a, b = _tile_scan(av[k * SUBLANES:(k + 1) * SUBLANES], bv[k * SUBLANES:(k + 1) * SUBLANES],
                                  dirn == 1)
                h = a * carry + b
                carry = h[SUBLANES - 1:SUBLANES] if dirn == 0 else h[0:1]
                hs[k] = h
            if emit is not None:
                emit(base, jnp.concatenate(hs, axis=0))
            return carry

        carry_ref[...] = lax.fori_loop(0, ng, body, carry_ref[...])

    def run(dirn):
        first = (j == 0) if dirn == 0 else (j == nc)
        ci = j if dirn == 0 else 2 * nc - 1 - j

        @pl.when(first)
        def _():
            zeros = jnp.zeros((SUBLANES, xe_ref.shape[1]), F32)
            xe_ref[0:SUBLANES] = zeros
            xe_ref[SUBLANES:SUBLANES + c_len] = xac_ref[0]
            xe_ref[SUBLANES + c_len:2 * SUBLANES + c_len] = zeros
            gates(conv(c_len), c_len, dirn)
            carry_ref[...] = jnp.zeros(carry_ref.shape, F32)
            scan(c_len, dirn, None)

        r0 = pl.multiple_of(ci * tc, tc)
        if dirn == 0:
            p0 = pl.multiple_of(jnp.maximum(r0 - SUBLANES, 0), SUBLANES)
            n0 = pl.multiple_of(jnp.minimum(r0 + tc, t_len - SUBLANES), SUBLANES)
            prev = xa_ref[0, pl.ds(p0, SUBLANES), :]
            nxt = xa_ref[0, pl.ds(n0, SUBLANES), :]
            xe_ref[0:SUBLANES] = jnp.where(ci > 0, prev, 0.0)
            xe_ref[SUBLANES:SUBLANES + tc] = xa_ref[0, pl.ds(r0, tc), :]
            xe_ref[SUBLANES + tc:2 * SUBLANES + tc] = jnp.where(ci < nc - 1, nxt, 0.0)
            u = conv(tc)
            u_ref[pl.ds(r0, tc), :] = u
        else:
            u = u_ref[pl.ds(r0, tc), :]
        gates(u, tc, dirn)

        if dirn == 0:
            def emit(base, h):
                hf_ref[pl.ds(r0 + base, SCAN_ROWS), :] = h
        else:
            def emit(base, h):
                tot = hf_ref[pl.ds(r0 + base, SCAN_ROWS), :] + h
                ga = ga_ref[0, pl.ds(base, SCAN_ROWS), :]
                y_ref[0, pl.ds(base, SCAN_ROWS), :] = (tot * jax.nn.gelu(ga)).astype(y_ref.dtype)
        scan(tc, dirn, emit)

    @pl.when(j < nc)
    def _():
        run(0)

    @pl.when(j >= nc)
    def _():
        run(1)


def _lru_call(xg, xg_c, conv_w, conv_b, wg, bg, lam):
    b, t, _ = xg.shape
    c_len = xg_c.shape[1]
    d_lru = conv_w.shape[1]
    tc = min(t, 256)
    nc = t // tc
    assert c_len <= tc and c_len % SCAN_ROWS == 0 and tc % SCAN_ROWS == 0

    def chunk_map(bi, j):
        return (bi, jnp.where(j < nc, nc - 1, 2 * nc - 1 - j), 0)

    def ga_map(bi, j):
        return (bi, jnp.where(j < nc, nc - 1, 2 * nc - 1 - j), 1)

    return pl.pallas_call(
        functools.partial(_lru_kernel, t_len=t, c_len=c_len, tc=tc, nc=nc),
        grid=(b, 2 * nc),
        in_specs=[
            pl.BlockSpec((1, t, d_lru), lambda bi, j: (bi, 0, 0)),
            pl.BlockSpec((1, tc, d_lru), ga_map),
            pl.BlockSpec((1, c_len, d_lru), lambda bi, j: (bi, 0, 0)),
            pl.BlockSpec(conv_w.shape, lambda bi, j: (0, 0)),
            pl.BlockSpec((1, d_lru), lambda bi, j: (0, 0)),
            pl.BlockSpec(wg.shape, lambda bi, j: (0, 0, 0, 0)),
            pl.BlockSpec(bg.shape, lambda bi, j: (0, 0, 0, 0)),
            pl.BlockSpec(lam.shape, lambda bi, j: (0, 0)),
        ],
        out_specs=pl.BlockSpec((1, tc, d_lru), chunk_map),
        out_shape=jax.ShapeDtypeStruct((b, t, d_lru), BF16),
        scratch_shapes=[
            pltpu.VMEM((t, d_lru), F32),
            pltpu.VMEM((t, d_lru), F32),
            pltpu.VMEM((tc + 2 * SUBLANES, d_lru), F32),
            pltpu.VMEM((tc, d_lru), F32),
            pltpu.VMEM((tc, d_lru), F32),
            pltpu.VMEM((1, d_lru), F32),
        ],
        compiler_params=_cparams("parallel", "arbitrary"),
        name="lru",
    )(xg, xg, xg_c, conv_w, conv_b.reshape(1, d_lru), wg, bg, lam)


def _ret_kernel(q_ref, k_ref, v_ref, g_ref, kc_ref, vc_ref, cos_ref, sin_ref, dm_ref, rt_ref, cwt_ref, gc_ref,
                y_ref, qr_ref, kr_ref, sf_ref, sb_ref, *, nc, scale):
    C = RET_CHUNK
    dh = q_ref.shape[2]
    def rope(xv, cs, sn):
        return xv * cs + pltpu.roll(xv, dh // 2, 1) * sn

    xi_f, xi_b, ze_f, ze_b = rt_ref[0, 0], rt_ref[0, 1], rt_ref[0, 2], rt_ref[0, 3]
    g_chunk = gc_ref[0]
    g_full = jnp.concatenate([g_chunk] * (dh // SUBLANES), axis=0)

    kc = (kc_ref[0] * scale).T.astype(BF16)
    vc = vc_ref[0]
    s_f0 = jnp.dot(kc, (vc * cwt_ref[0, 0]).astype(BF16), preferred_element_type=F32)
    s_b0 = jnp.dot(kc, (vc * cwt_ref[0, 1]).astype(BF16), preferred_element_type=F32)

    def p1(n, s):
        r = pl.multiple_of(n * C, C)
        cs, sn = cos_ref[pl.ds(r, C), :], sin_ref[pl.ds(r, C), :]
        k = rope(k_ref[0, pl.ds(r, C), :], cs, sn) * scale
        qr_ref[pl.ds(r, C), :] = rope(q_ref[0, pl.ds(r, C), :], cs, sn)
        kr_ref[pl.ds(r, C), :] = k.astype(BF16)
        v = v_ref[0, pl.ds(r, C), :]
        vz = jnp.concatenate([v * ze_f, v * ze_b], axis=1).astype(BF16)
        kv = jnp.dot(k.T.astype(BF16), vz, preferred_element_type=F32)
        sf_ref[n] = s
        sb_ref[n] = kv[:, dh:]
        return g_full * s + kv[:, :dh]

    lax.fori_loop(0, nc, p1, s_f0, unroll=min(RET_UNROLL, nc))

    def p1b(i, s):
        n = nc - 1 - i
        kvb = sb_ref[n]
        sb_ref[n] = s
        return g_full * s + kvb

    lax.fori_loop(0, nc, p1b, s_b0)

    def p2(n, carry):
        r = pl.multiple_of(n * C, C)
        q = qr_ref[pl.ds(r, C), :]
        kb = kr_ref[pl.ds(r, C), :]
        vb = v_ref[0, pl.ds(r, C), :].astype(BF16)
        sc = lax.dot_general(q.astype(BF16), kb, NT_DIMS, preferred_element_type=F32) * dm_ref[0]
        o = jnp.dot(sc.astype(BF16), vb, preferred_element_type=F32)
        qx = jnp.concatenate([q * xi_f, q * xi_b], axis=1).astype(BF16)
        st = jnp.concatenate([sf_ref[n], sb_ref[n]], axis=0).astype(BF16)
        o = o + jnp.dot(qx, st, preferred_element_type=F32)
        o = o * lax.rsqrt(jnp.mean(o * o, axis=-1, keepdims=True) + EPS)
        y_ref[0, pl.ds(r, C), :] = (o * _silu(g_ref[0, pl.ds(r, C), :])).astype(y_ref.dtype)
        return carry

    lax.fori_loop(0, nc, p2, 0, unroll=min(RET_UNROLL, nc))


def _ret_tables(t, c_len, dh):
    C = RET_CHUNK
    hh = jnp.arange(RET_HEADS, dtype=F32)
    log_g = jnp.log1p(-jnp.exp2(-(5.0 + 2.0 * hh)))
    idx = jnp.arange(C, dtype=F32)
    dm = jnp.exp(jnp.abs(idx[:, None] - idx[None, :])[None] * log_g[:, None, None])

    def lanes(e):
        return jnp.broadcast_to(jnp.exp(e[None, :] * log_g[:, None])[:, :, None], (RET_HEADS, e.shape[0], dh))

    rt = jnp.stack([lanes(idx + 1.0), lanes(C - idx), lanes(C - 1.0 - idx), lanes(idx)], axis=1)
    pos = jnp.arange(c_len, dtype=F32)
    cwt = jnp.stack([lanes(c_len - 1.0 - pos), lanes(pos)], axis=1)
    gc = jnp.broadcast_to(jnp.exp(C * log_g)[:, None, None], (RET_HEADS, SUBLANES, dh))
    rows = t // GRID_W
    row = jnp.repeat(jnp.arange(rows, dtype=F32), GRID_W)
    col = jnp.tile(jnp.arange(GRID_W, dtype=F32), rows)
    q4 = dh // 4
    freqs = ROPE_BASE ** (-jnp.arange(q4, dtype=F32) / q4)
    ang_r = row[:, None] * freqs[None, :]
    ang_c = col[:, None] * freqs[None, :]
    cos = jnp.concatenate([jnp.cos(ang_r), jnp.cos(ang_c)] * 2, axis=1)
    sin = jnp.concatenate([-jnp.sin(ang_r), -jnp.sin(ang_c), jnp.sin(ang_r), jnp.sin(ang_c)], axis=1)
    return dm, rt, cwt, gc, cos, sin


def _pair_apart(w_in, n_xg):
    d, n_all = w_in.shape
    dh = (n_all - n_xg) // (4 * RET_HEADS)
    qk = w_in[:, n_xg:n_xg + 2 * RET_HEADS * dh].reshape(d, 2 * RET_HEADS, 2, 2, dh // 4)
    qk = jnp.swapaxes(qk, 2, 3).reshape(d, 2 * RET_HEADS * dh)
    return jnp.concatenate([w_in[:, :n_xg], qk, w_in[:, n_xg + 2 * RET_HEADS * dh:]], axis=1)


def _ret_call(qkvg, qkvg_c):
    b, t, w = qkvg.shape
    c_len = qkvg_c.shape[1]
    dh = w // (4 * RET_HEADS)
    nc = t // RET_CHUNK
    dm, rt, cwt, gc, cos, sin = _ret_tables(t, c_len, dh)
    H = RET_HEADS

    def col(off):
        return lambda bi, h: (bi, 0, off + h)

    return pl.pallas_call(
        functools.partial(_ret_kernel, nc=nc, scale=dh ** -0.5),
        grid=(b, H),
        in_specs=[
            pl.BlockSpec((1, t, dh), col(0)),
            pl.BlockSpec((1, t, dh), col(H)),
            pl.BlockSpec((1, t, dh), col(2 * H)),
            pl.BlockSpec((1, t, dh), col(3 * H)),
            pl.BlockSpec((1, c_len, dh), col(H)),
            pl.BlockSpec((1, c_len, dh), col(2 * H)),
            pl.BlockSpec((t, dh), lambda bi, h: (0, 0)),
            pl.BlockSpec((t, dh), lambda bi, h: (0, 0)),
            pl.BlockSpec((1, RET_CHUNK, RET_CHUNK), lambda bi, h: (h, 0, 0)),
            pl.BlockSpec((1, 4, RET_CHUNK, dh), lambda bi, h: (h, 0, 0, 0)),
            pl.BlockSpec((1, 2, c_len, dh), lambda bi, h: (h, 0, 0, 0)),
            pl.BlockSpec((1, SUBLANES, dh), lambda bi, h: (h, 0, 0)),
        ],
        out_specs=pl.BlockSpec((1, t, dh), col(0)),
        out_shape=jax.ShapeDtypeStruct((b, t, H * dh), BF16),
        scratch_shapes=[
            pltpu.VMEM((t, dh), F32),
            pltpu.VMEM((t, dh), BF16),
            pltpu.VMEM((nc, dh, dh), F32),
            pltpu.VMEM((nc, dh, dh), F32),
        ],
        compiler_params=_cparams("parallel", "arbitrary"),
        name="retention",
    )(qkvg, qkvg, qkvg, qkvg, qkvg_c, qkvg_c, cos, sin, dm, rt, cwt, gc)


def _route(logits_t, bias):
    e, n = logits_t.shape
    per = e // N_GROUPS
    s = jax.nn.sigmoid(logits_t)
    sb = s + bias
    neg = -jnp.inf
    g3 = sb.reshape(N_GROUPS, per, n)
    it3 = lax.broadcasted_iota(jnp.int32, g3.shape, 1)
    m1 = jnp.max(g3, axis=1, keepdims=True)
    i1 = jnp.min(jnp.where(g3 == m1, it3, per), axis=1, keepdims=True)
    m2 = jnp.max(jnp.where(it3 == i1, neg, g3), axis=1, keepdims=True)
    gs = (m1 + m2).reshape(N_GROUPS, n)
    itg = lax.broadcasted_iota(jnp.int32, gs.shape, 0)
    gsel = jnp.zeros(gs.shape, jnp.bool_)
    for _ in range(TOP_GROUPS):
        m = jnp.max(gs, axis=0, keepdims=True)
        i = jnp.min(jnp.where(gs == m, itg, N_GROUPS), axis=0, keepdims=True)
        hit = itg == i
        gsel = gsel | hit
        gs = jnp.where(hit, neg, gs)
    gmask = jnp.broadcast_to(gsel.reshape(N_GROUPS, 1, n), (N_GROUPS, per, n)).reshape(e, n)
    v = jnp.where(gmask, sb, neg)
    ite = lax.broadcasted_iota(jnp.int32, v.shape, 0)
    sel = jnp.zeros(v.shape, jnp.bool_)
    for _ in range(TOP_K):
        m = jnp.max(v, axis=0, keepdims=True)
        i = jnp.min(jnp.where(v == m, ite, e), axis=0, keepdims=True)
        hit = ite == i
        sel = sel | hit
        v = jnp.where(hit, neg, v)
    picked = jnp.where(sel, s, 0.0)
    return picked / jnp.sum(picked, axis=0, keepdims=True) * ROUTE_SCALE


def _outproj_kernel(yl_ref, yr_ref, x_ref, g1_ref, sh_ref, sc_ref, nf_ref, wo_ref, wr_ref, rb_ref,
                    x1_ref, h2_ref, gt_ref, cnt_ref):
    dl = yl_ref.shape[2]
    ts = gt_ref.shape[1] // cnt_ref.shape[0]
    w = wr_ref[...]
    w_hi = w.astype(BF16)
    r1 = w - w_hi.astype(F32)
    w_mid = r1.astype(BF16)
    w_lo = (r1 - w_mid.astype(F32)).astype(BF16)
    for k in range(cnt_ref.shape[0]):
        rows = pl.ds(k * ts, ts)
        mx = jnp.dot(yl_ref[0, rows, :], wo_ref[0:dl, :], preferred_element_type=F32)
        mx = mx + jnp.dot(yr_ref[0, rows, :], wo_ref[dl:, :], preferred_element_type=F32)
        x1 = x_ref[0, rows, :] + g1_ref[0] * mx
        x1_ref[0, rows, :] = x1
        h = _rms_mod(x1, nf_ref[...], sh_ref[0], sc_ref[0])
        h_hi = h.astype(BF16)
        h2_ref[0, rows, :] = h_hi
        h_lo = (h - h_hi.astype(F32)).astype(BF16)
        small = (_dot32(h_hi, w_lo) + _dot32(h_lo, w_mid)) + (_dot32(h_hi, w_mid) + _dot32(h_lo, w_hi))
        logits = small + _dot32(h_hi, w_hi)
        ne = logits.shape[1]
        logits_t = jnp.concatenate([logits, jnp.zeros((ts, LANES - ne), F32)], axis=1).T[:ne, :]
        gates = _route(logits_t, rb_ref[...])
        gt_ref[:, rows] = gates
        hit = (gates > 0.0).astype(BF16)
        cnt_ref[k] = lax.dot_general(jnp.ones((SUBLANES, ts), BF16), hit, NT_DIMS, preferred_element_type=F32)


def _outproj_call(y_lru, y_ret, x, g1, sh2, sc2, norm_ffn, w_out_bf, w_router, b_router):
    b, t, d = x.shape
    dl = y_lru.shape[2]
    e = w_router.shape[1]
    ts = min(t, MOE_TILE)
    tt = min(t, OUTPROJ_TILES * ts)
    nt = t // tt
    per = tt // ts
    mod_map = lambda bi, i: (bi, 0, 0)
    return pl.pallas_call(
        _outproj_kernel,
        grid=(b, nt),
        in_specs=[
            pl.BlockSpec((1, tt, dl), lambda bi, i: (bi, i, 0)),
            pl.BlockSpec((1, tt, d - dl), lambda bi, i: (bi, i, 0)),
            pl.BlockSpec((1, tt, d), lambda bi, i: (bi, i, 0)),
            pl.BlockSpec((1, 1, d), mod_map),
            pl.BlockSpec((1, 1, d), mod_map),
            pl.BlockSpec((1, 1, d), mod_map),
            pl.BlockSpec((1, d), lambda bi, i: (0, 0)),
            pl.BlockSpec((d, d), lambda bi, i: (0, 0)),
            pl.BlockSpec((d, e), lambda bi, i: (0, 0)),
            pl.BlockSpec((e, 1), lambda bi, i: (0, 0)),
        ],
        out_specs=[
            pl.BlockSpec((1, tt, d), lambda bi, i: (bi, i, 0)),
            pl.BlockSpec((1, tt, d), lambda bi, i: (bi, i, 0)),
            pl.BlockSpec((e, tt), lambda bi, i: (0, bi * nt + i)),
            pl.BlockSpec((per, SUBLANES, e), lambda bi, i: (bi * nt + i, 0, 0)),
        ],
        out_shape=[
            jax.ShapeDtypeStruct((b, t, d), F32),
            jax.ShapeDtypeStruct((b, t, d), BF16),
            jax.ShapeDtypeStruct((e, b * t), F32),
            jax.ShapeDtypeStruct((b * nt * per, SUBLANES, e), F32),
        ],
        compiler_params=_cparams("parallel", "arbitrary"),
        name="outproj_route",
    )(y_lru, y_ret, x, g1, sh2, sc2, norm_ffn.reshape(1, d), w_out_bf, w_router, b_router.reshape(e, 1))


MOE_TILE = 256
MOE_CHUNK = 16
MOE_RB = 512
MOE_ROWS_COMMON = 2048
MOE_BM = 2048
MOE_EXPERT_ROWS = 1024
OUTPROJ_TILES = 2
NO_RANK = 512.0

TN_DIMS = (((0,), (0,)), ((), ()))
TAB_LOFF, TAB_GBASE, TAB_CPAD, TAB_NROWS, TAB_EEND = range(5)


def _round_up(v, m):
    return -(-v // m) * m


def _tables_kernel(cnt_ref, loff_ref, gbase_ref, cpad_ref, eend_ref, blk_ref, nused_ref, ctab_ref, *, nb):
    cnt = cnt_ref[...]
    ns, ne = cnt.shape
    cpad = jnp.ceil(cnt * (1.0 / MOE_CHUNK)) * MOE_CHUNK
    upper = (lax.broadcasted_iota(jnp.int32, (ne, ne), 0) < lax.broadcasted_iota(jnp.int32, (ne, ne), 1)).astype(F32)
    lower = (lax.broadcasted_iota(jnp.int32, (ns, ns), 1) < lax.broadcasted_iota(jnp.int32, (ns, ns), 0)).astype(F32)
    loff = jnp.dot(cpad, upper, precision=HIGHEST, preferred_element_type=F32)
    before = jnp.dot(lower, cpad, precision=HIGHEST, preferred_element_type=F32)
    tot = jnp.sum(cpad, axis=0, keepdims=True)
    epad = jnp.ceil(tot * (1.0 / MOE_BM)) * MOE_BM
    epad8 = jnp.broadcast_to(epad, (SUBLANES, ne))
    ebase = jnp.dot(epad8, upper, precision=HIGHEST, preferred_element_type=F32)[0:1]
    eend = ebase + epad
    loff_ref[...] = loff
    gbase_ref[...] = ebase + before
    cpad_ref[...] = cpad
    eend_ref[...] = eend
    start = lax.broadcasted_iota(jnp.int32, (nb, ne), 0).astype(F32) * MOE_BM
    blk = jnp.sum((eend <= start).astype(F32), axis=1, keepdims=True)
    blk_ref[...] = jnp.minimum(blk, ne - 1.0)
    nused_ref[...] = jnp.sum(epad, axis=1, keepdims=True) * (1.0 / MOE_BM)
    nch = ctab_ref.shape[1]
    cidx = lax.broadcasted_iota(jnp.int32, (ns, nch), 1).astype(F32)
    shift = (ebase + before - loff) * (1.0 / MOE_CHUNK)
    ctab = jnp.full((ns, nch), -1.0, F32)
    for e in range(ne):
        lo = loff[:, e:e + 1] * (1.0 / MOE_CHUNK)
        hi = lo + cpad[:, e:e + 1] * (1.0 / MOE_CHUNK)
        ctab = jnp.where((cidx >= lo) & (cidx < hi), cidx + shift[:, e:e + 1], ctab)
    ctab_ref[...] = ctab


def _tables_call(cnt, nb, nch):
    ns, ne = cnt.shape
    f = lambda shape: jax.ShapeDtypeStruct(shape, F32)
    return pl.pallas_call(
        functools.partial(_tables_kernel, nb=nb),
        out_shape=[f((ns, ne)), f((ns, ne)), f((ns, ne)), f((1, ne)), f((nb, 1)), f((1, 1)), f((ns, nch))],
        compiler_params=pltpu.CompilerParams(vmem_limit_bytes=VMEM_LIMIT_BYTES),
        name="moe_tables",
    )(cnt)


def _tile_rank_rhs(gt, lcol):
    ne, ts = gt.shape
    hit = gt > 0.0
    before = (lax.broadcasted_iota(jnp.int32, (ts, ts), 0) < lax.broadcasted_iota(jnp.int32, (ts, ts), 1)).astype(BF16)
    rank = jnp.dot(hit.astype(BF16), before, preferred_element_type=F32)
    rm = jnp.where(hit, rank, NO_RANK)
    q = jnp.broadcast_to(lcol * (1.0 / MOE_CHUNK), (ne, LANES))
    return jnp.concatenate([rm, q], axis=1).astype(BF16)


def _onehot_rows(r0, rhs, lrow, crow, ts):
    ne = lrow.shape[1]
    rows = (r0 + lax.broadcasted_iota(jnp.int32, (MOE_RB, ne), 0)).astype(F32)
    seg = ((rows >= lrow) & (rows < crow)).astype(BF16)
    ex = jnp.dot(seg, rhs, preferred_element_type=F32)
    rowl = (r0 + lax.broadcasted_iota(jnp.int32, (MOE_RB, LANES), 0)).astype(F32)
    tgt = rowl - MOE_CHUNK * ex[:, ts:ts + LANES]
    p = ex[:, :ts] == jnp.concatenate([tgt] * (ts // LANES), axis=1)
    return p, (ex[:, ts + LANES:] if rhs.shape[1] > ts + LANES else None)


def _chunk_copy(buf, slot, c, row, hbm, sem, to_hbm):
    loc = buf.at[slot, pl.ds(c * MOE_CHUNK, MOE_CHUNK)]
    glob = hbm.at[pl.ds(pl.multiple_of(row, MOE_CHUNK), MOE_CHUNK)]
    return pltpu.make_async_copy(loc, glob, sem.at[slot]) if to_hbm else pltpu.make_async_copy(glob, loc, sem.at[slot])


def _wait_chunks(n, buf, slot, hbm, sem, to_hbm):
    for _ in range(n):
        _chunk_copy(buf, slot, 0, 0, hbm, sem, to_hbm).wait()


def _wait_chunks_dyn(n, buf, slot, hbm, sem, to_hbm):
    def body(i, c):
        _chunk_copy(buf, slot, 0, 0, hbm, sem, to_hbm).wait()
        return c

    lax.fori_loop(0, n, body, 0)


def _dispatch_kernel(tab_ref, ctab_ref, h_ref, gt_ref, lrow_ref, crow_ref, lcol_ref, xs_hbm, xbuf, zbuf, sem, tailf,
                     *, ns, trash0):
    s = pl.program_id(0)
    slot = s % 2
    ts = h_ref.shape[0]
    ne = gt_ref.shape[0]
    rows_max = xbuf.shape[1]
    nch = rows_max // MOE_CHUNK
    rows_c = min(rows_max, MOE_ROWS_COMMON)
    nch_c = rows_c // MOE_CHUNK

    def wait_slot(sl):
        _wait_chunks(nch_c, xbuf, sl, xs_hbm, sem, True)
        if nch > nch_c:
            @pl.when(tailf[sl] == 1)
            def _():
                _wait_chunks(nch - nch_c, xbuf, sl, xs_hbm, sem, True)

    @pl.when(s >= 2)
    def _():
        wait_slot(slot)

    rhs = _tile_rank_rhs(gt_ref[...], lcol_ref[0])

    def rows_block(r0):
        p, _ = _onehot_rows(r0, rhs, lrow_ref[0], crow_ref[0], ts)
        xbuf[slot, r0:r0 + MOE_RB, :] = jnp.dot(p.astype(BF16), h_ref[...],
                                                preferred_element_type=F32).astype(xbuf.dtype)

    def copy_chunks(c0, c1):
        for c in range(c0, c1):
            g = ctab_ref[0, 0, c]
            row = jnp.where(g >= 0, g * MOE_CHUNK, trash0 + slot * rows_max + c * MOE_CHUNK)
            _chunk_copy(xbuf, slot, c, row, xs_hbm, sem, True).start()

    for r0 in range(0, rows_c, MOE_RB):
        rows_block(r0)
    copy_chunks(0, nch_c)
    if nch > nch_c:
        long_tile = ctab_ref[0, 0, nch_c] >= 0
        tailf[slot] = long_tile.astype(jnp.int32)

        @pl.when(long_tile)
        def _():
            for r0 in range(rows_c, rows_max, MOE_RB):
                rows_block(r0)
            copy_chunks(nch_c, nch)

    @pl.when(s == ns - 1)
    def _():
        zbuf[...] = jnp.zeros(zbuf.shape, zbuf.dtype)

        def per_expert(e, total):
            first = tab_ref[0, TAB_GBASE, e] + tab_ref[0, TAB_CPAD, e]
            ntail = lax.shift_right_logical(tab_ref[0, TAB_EEND, e] - first, MOE_CHUNK.bit_length() - 1)

            def per_chunk(j, c2):
                dst = xs_hbm.at[pl.ds(pl.multiple_of(first + j * MOE_CHUNK, MOE_CHUNK), MOE_CHUNK)]
                pltpu.make_async_copy(zbuf, dst, sem.at[slot]).start()
                return c2

            lax.fori_loop(0, ntail, per_chunk, 0)
            return total + ntail

        ntail = lax.fori_loop(0, ne, per_expert, 0)
        _wait_chunks_dyn(ntail, xbuf, slot, xs_hbm, sem, True)
        wait_slot(slot)
        if ns >= 2:
            wait_slot(1 - slot)


def _dispatch_call(tab, ctab, h2, gt, lrow, crow, lcol, p_max, rows_max):
    n, d = h2.shape
    ne = gt.shape[0]
    ts = min(n, MOE_TILE)
    ns = n // ts
    nch = rows_max // MOE_CHUNK
    return pl.pallas_call(
        functools.partial(_dispatch_kernel, ns=ns, trash0=p_max),
        grid=(ns,),
        in_specs=[
            pl.BlockSpec((1, 5, ne), lambda s: (s, 0, 0), memory_space=pltpu.SMEM),
            pl.BlockSpec((1, 1, nch), lambda s: (s, 0, 0), memory_space=pltpu.SMEM),
            pl.BlockSpec((ts, d), lambda s: (s, 0)),
            pl.BlockSpec((ne, ts), lambda s: (0, s)),
            pl.BlockSpec((1, 1, ne), lambda s: (s, 0, 0)),
            pl.BlockSpec((1, 1, ne), lambda s: (s, 0, 0)),
            pl.BlockSpec((1, ne, 1), lambda s: (s, 0, 0)),
        ],
        out_specs=pl.BlockSpec(memory_space=pl.ANY),
        out_shape=jax.ShapeDtypeStruct((p_max + 2 * rows_max, d), BF16),
        scratch_shapes=[
            pltpu.VMEM((2, rows_max, d), BF16),
            pltpu.VMEM((MOE_CHUNK, d), BF16),
            pltpu.SemaphoreType.DMA((2,)),
            pltpu.SMEM((2,), jnp.int32),
        ],
        compiler_params=_cparams("arbitrary"),
        name="moe_dispatch",
    )(tab, ctab, h2, gt, lrow, crow, lcol)


def _experts_kernel(blk_ref, nused_ref, x_ref, w1_ref, w3_ref, w2_ref, y_ref):
    del blk_ref

    @pl.when(pl.program_id(0) < nused_ref[0])
    def _():
        for r0 in range(0, x_ref.shape[0], MOE_EXPERT_ROWS):
            xv = x_ref[r0:r0 + MOE_EXPERT_ROWS, :]
            hid = _silu(_dot32(xv, w1_ref[0])) * _dot32(xv, w3_ref[0])
            y_ref[r0:r0 + MOE_EXPERT_ROWS, :] = _dot32(hid.astype(BF16), w2_ref[0]).astype(y_ref.dtype)


def _experts_call(blk, nused, xs, w1, w3, w2):
    d = xs.shape[1]
    _, _, de = w1.shape
    nb = blk.shape[0]
    p_max = nb * MOE_BM
    row_map = lambda i, blk_r, nu_r: (jnp.maximum(jnp.minimum(i, nu_r[0] - 1), 0), 0)
    w_map = lambda i, blk_r, nu_r: (blk_r[i], 0, 0)
    return pl.pallas_call(
        _experts_kernel,
        grid_spec=pltpu.PrefetchScalarGridSpec(
            num_scalar_prefetch=2,
            grid=(nb,),
            in_specs=[
                pl.BlockSpec((MOE_BM, d), row_map),
                pl.BlockSpec((1, d, de), w_map),
                pl.BlockSpec((1, d, de), w_map),
                pl.BlockSpec((1, de, d), w_map),
            ],
            out_specs=pl.BlockSpec((MOE_BM, d), row_map),
        ),
        out_shape=jax.ShapeDtypeStruct((p_max, d), BF16),
        compiler_params=_cparams("arbitrary"),
        name="moe_experts",
    )(blk, nused, xs, w1, w3, w2)


def _combine_kernel(ctab_ref, ctabn_ref, gt_ref, lrow_ref, crow_ref, lcol_ref, h_ref, x1_ref, g2_ref,
                    s1_ref, s3_ref, s2_ref, nfin_ref, ys_hbm, o_ref, ybuf, acc_ref, sem, *, ns):
    s = pl.program_id(0)
    slot = s % 2
    ts = h_ref.shape[0]
    rows_max = ybuf.shape[1]
    nch = rows_max // MOE_CHUNK
    rows_c = min(rows_max, MOE_ROWS_COMMON)
    nch_c = rows_c // MOE_CHUNK

    def chunk_range(tab, sl, c0, c1):
        for c in range(c0, c1):
            row = jnp.maximum(tab[0, 0, c], 0) * MOE_CHUNK
            _chunk_copy(ybuf, sl, c, row, ys_hbm, sem, False).start()

    def gather(tab, sl):
        chunk_range(tab, sl, 0, nch_c)
        if nch > nch_c:
            @pl.when(tab[0, 0, nch_c] >= 0)
            def _():
                chunk_range(tab, sl, nch_c, nch)

    def wait(tab, sl):
        _wait_chunks(nch_c, ybuf, sl, ys_hbm, sem, False)
        if nch > nch_c:
            @pl.when(tab[0, 0, nch_c] >= 0)
            def _():
                _wait_chunks(nch - nch_c, ybuf, sl, ys_hbm, sem, False)

    @pl.when(s == 0)
    def _():
        gather(ctab_ref, 0)

    gather(ctabn_ref, 1 - slot)

    h = h_ref[...]
    hs = _silu(jnp.dot(h, s1_ref[...], preferred_element_type=F32)) * jnp.dot(h, s3_ref[...], preferred_element_type=F32)
    acc = jnp.dot(hs.astype(BF16), s2_ref[...], preferred_element_type=F32)

    gt = gt_ref[...]
    rhs = jnp.concatenate([_tile_rank_rhs(gt, lcol_ref[0]), gt.astype(BF16)], axis=1)
    wait(ctab_ref, slot)

    def rows_block(r0):
        p, gexp = _onehot_rows(r0, rhs, lrow_ref[0], crow_ref[0], ts)
        gm = jnp.where(p, gexp, 0.0).astype(BF16)
        return lax.dot_general(gm, ybuf[slot, r0:r0 + MOE_RB, :], TN_DIMS, preferred_element_type=F32)

    for r0 in range(0, rows_c, MOE_RB):
        acc = acc + rows_block(r0)
    if nch > nch_c:
        acc_ref[...] = acc

        @pl.when(ctab_ref[0, 0, nch_c] >= 0)
        def _():
            for r0 in range(rows_c, rows_max, MOE_RB):
                acc_ref[...] += rows_block(r0)

        acc = acc_ref[...]
    xo = x1_ref[...] + g2_ref[0] * acc
    o_ref[...] = xo * lax.rsqrt(jnp.mean(xo * xo, axis=-1, keepdims=True) + EPS) * nfin_ref[...]

    @pl.when(s == ns - 1)
    def _():
        wait(ctabn_ref, 1 - slot)


def _combine_call(ctab, gt, lrow, crow, lcol, h2, x1, g2, ws1, ws3, ws2, norm_final, ys, rows_max, t):
    n, d = h2.shape
    ne = gt.shape[0]
    ts = min(n, MOE_TILE)
    ns = n // ts
    per_b = t // ts
    nch = rows_max // MOE_CHUNK
    return pl.pallas_call(
        functools.partial(_combine_kernel, ns=ns),
        grid=(ns,),
        in_specs=[
            pl.BlockSpec((1, 1, nch), lambda s: (s, 0, 0), memory_space=pltpu.SMEM),
            pl.BlockSpec((1, 1, nch), lambda s: (jnp.minimum(s + 1, ns - 1), 0, 0), memory_space=pltpu.SMEM),
            pl.BlockSpec((ne, ts), lambda s: (0, s)),
            pl.BlockSpec((1, 1, ne), lambda s: (s, 0, 0)),
            pl.BlockSpec((1, 1, ne), lambda s: (s, 0, 0)),
            pl.BlockSpec((1, ne, 1), lambda s: (s, 0, 0)),
            pl.BlockSpec((ts, d), lambda s: (s, 0)),
            pl.BlockSpec((ts, d), lambda s: (s, 0)),
            pl.BlockSpec((1, 1, d), lambda s: (s // per_b, 0, 0)),
            pl.BlockSpec(ws1.shape, lambda s: (0, 0)),
            pl.BlockSpec(ws3.shape, lambda s: (0, 0)),
            pl.BlockSpec(ws2.shape, lambda s: (0, 0)),
            pl.BlockSpec((1, d), lambda s: (0, 0)),
            pl.BlockSpec(memory_space=pl.ANY),
        ],
        out_specs=pl.BlockSpec((ts, d), lambda s: (s, 0)),
        out_shape=jax.ShapeDtypeStruct((n, d), F32),
        scratch_shapes=[
            pltpu.VMEM((2, rows_max, d), BF16),
            pltpu.VMEM((ts, d), F32),
            pltpu.SemaphoreType.DMA((2,)),
        ],
        compiler_params=_cparams("arbitrary"),
        name="moe_combine",
    )(ctab, ctab, gt, lrow, crow, lcol, h2, x1, g2, ws1, ws3, ws2, norm_final.reshape(1, d), ys)


def _moe_call(h2, gt, cnt, x1, g2, w1, w3, w2, ws1, ws3, ws2, norm_final, t):
    n, d = h2.shape
    ne = gt.shape[0]
    ts = min(n, MOE_TILE)
    ns = n // ts
    rows_max = _round_up(TOP_K * ts + ne * (MOE_CHUNK - 1), MOE_RB)
    p_max = _round_up(n * TOP_K + ns * ne * (MOE_CHUNK - 1) + ne * (MOE_BM - 1), MOE_BM)
    nb = p_max // MOE_BM

    nch = rows_max // MOE_CHUNK
    loff, gbase, cpad, eend, blk, nused, ctab = _tables_call(cnt[:, 0, :], nb, nch)
    nrows = loff[:, ne - 1:] + cpad[:, ne - 1:]
    tab = jnp.stack([loff, gbase, cpad, jnp.broadcast_to(nrows, (ns, ne)), jnp.broadcast_to(eend, (ns, ne))],
                    axis=1).astype(jnp.int32)
    ctab = ctab.astype(jnp.int32).reshape(ns, 1, nch)
    lrow = loff.reshape(ns, 1, ne)
    crow = (loff + cpad).reshape(ns, 1, ne)
    lcol = loff.reshape(ns, ne, 1)

    xs = _dispatch_call(tab, ctab, h2, gt, lrow, crow, lcol, p_max, rows_max)
    ys = _experts_call(blk.reshape(nb).astype(jnp.int32), nused.reshape(1).astype(jnp.int32), xs, w1, w3, w2)
    return _combine_call(ctab, gt, lrow, crow, lcol, h2, x1, g2, ws1, ws3, ws2, norm_final, ys, rows_max, t)


def _block_diag(w):
    nb, n, _ = w.shape
    eye = jnp.eye(nb, dtype=w.dtype)
    return (eye[:, None, :, None] * w[:, :, None, :]).reshape(nb * n, nb * n)


def _lru_gate_weights(w_rec, b_rec, w_inp, b_inp):
    d_lru = b_rec.shape[1]
    half = d_lru // 2
    ws, bs = [], []
    for dirn in range(2):
        wr, wi = _block_diag(w_rec[dirn]), _block_diag(w_inp[dirn])
        ws.append(jnp.stack([jnp.concatenate([wr[s:s + half, s:s + half], wi[s:s + half, s:s + half]], axis=1)
                             for s in (0, half)]))
        bs.append(jnp.stack([jnp.concatenate([b_rec[dirn, s:s + half], b_inp[dirn, s:s + half]])[None, :]
                             for s in (0, half)]))
    return jnp.stack(ws).astype(BF16), jnp.stack(bs).astype(F32)


def kernel(x, c, ctx, c_ctx, w_mod, b_mod, norm_mix, norm_ffn, w_in, conv_w, conv_b, lru_w_rec, lru_b_rec, lru_w_in, lru_b_in, lru_lambda, w_out, w_router, b_router, w_exp_gate, w_exp_up, w_exp_down, w_sh_gate, w_sh_up, w_sh_down, norm_final):
    b, t, d = x.shape
    depth = w_mod.shape[0]
    assert depth == 1, "context-stream update for deeper stacks is not implemented"
    l = 0
    d_lru = conv_w.shape[2]
    n_xg = 2 * d_lru

    rows = -(-(b + 1) // SUBLANES) * SUBLANES
    cc = jnp.concatenate([c, c_ctx[None, :], jnp.zeros((rows - b - 1, d), F32)], axis=0)
    mod = _mod_call(cc, w_mod[l], b_mod[l]).reshape(rows, 1, -1)
    sh1, sc1, g1, sh2, sc2, g2 = (mod[:, :, i * d:(i + 1) * d] for i in range(6))

    w_in_bf = _pair_apart(w_in[l], n_xg).astype(BF16)
    gain_mix = norm_mix[l].reshape(1, d)
    xg, qkvg = _inproj_call(x, sh1[:b], sc1[:b], gain_mix, w_in_bf, n_xg, True)
    xg_c, qkvg_c = _inproj_call(ctx, sh1[b:b + 1], sc1[b:b + 1], gain_mix, w_in_bf, n_xg, False)

    wg, bg = _lru_gate_weights(lru_w_rec[l], lru_b_rec[l], lru_w_in[l], lru_b_in[l])
    y_lru = _lru_call(xg, xg_c, conv_w[l], conv_b[l], wg, bg, lru_lambda[l])
    y_ret = _ret_call(qkvg, qkvg_c)

    x1, h2, gt, cnt = _outproj_call(y_lru, y_ret, x, g1[:b], sh2[:b], sc2[:b], norm_ffn[l], w_out[l].astype(BF16),
                                    w_router[l], b_router[l])

    out = _moe_call(h2.reshape(b * t, d), gt, cnt, x1.reshape(b * t, d), g2[:b],
                    w_exp_gate[l].astype(BF16), w_exp_up[l].astype(BF16), w_exp_down[l].astype(BF16),
                    w_sh_gate[l].astype(BF16), w_sh_up[l].astype(BF16), w_sh_down[l].astype(BF16), norm_final, t)
    return out.reshape(b, t, d)
```

```python
import functools

import jax
import jax.numpy as jnp
from jax import lax
from jax.experimental import pallas as pl
from jax.experimental.pallas import tpu as pltpu

F32 = jnp.float32
BF16 = jnp.bfloat16
HIGHEST = lax.Precision.HIGHEST

EPS = 1e-6
GRID_W = 64
LRU_C = 8.0
RET_HEADS = 4
RET_CHUNK = 128
RET_UNROLL = 32
ROPE_BASE = 10000.0
N_GROUPS = 8
TOP_GROUPS = 4
TOP_K = 6
ROUTE_SCALE = 2.5

LANES = 128
SUBLANES = 8
VMEM_LIMIT_BYTES = 56 * 1024 * 1024

NT_DIMS = (((1,), (1,)), ((), ()))


def _cparams(*sem):
    return pltpu.CompilerParams(dimension_semantics=sem, vmem_limit_bytes=VMEM_LIMIT_BYTES)


def _silu(v):
    return v * jax.nn.sigmoid(v)


def _dot32(a, b):
    return jnp.dot(a, b, preferred_element_type=F32)


def _rms_mod(x, gain, shift, scale):
    y = x * lax.rsqrt(jnp.mean(x * x, axis=-1, keepdims=True) + EPS) * gain
    return y * (1.0 + scale) + shift


def _mod_kernel(c_ref, w_ref, b_ref, o_ref):
    s = _silu(c_ref[...])
    o_ref[...] = jnp.dot(s, w_ref[...], precision=HIGHEST, preferred_element_type=F32) + b_ref[...]


def _mod_call(cc, w_mod, b_mod):
    rows, d = cc.shape
    n = w_mod.shape[1]
    tn = 512
    return pl.pallas_call(
        _mod_kernel,
        grid=(n // tn,),
        in_specs=[
            pl.BlockSpec((rows, d), lambda j: (0, 0)),
            pl.BlockSpec((d, tn), lambda j: (0, j)),
            pl.BlockSpec((1, tn), lambda j: (0, j)),
        ],
        out_specs=pl.BlockSpec((rows, tn), lambda j: (0, j)),
        out_shape=jax.ShapeDtypeStruct((rows, n), F32),
        compiler_params=_cparams("arbitrary"),
        name="mod",
    )(cc, w_mod, b_mod.reshape(1, n))


def _inproj_kernel(x_ref, sh_ref, sc_ref, g_ref, w_ref, xg_ref, qkvg_ref, *, n_xg):
    h = _rms_mod(x_ref[0], g_ref[...], sh_ref[0], sc_ref[0]).astype(BF16)
    xg_ref[0] = jnp.dot(h, w_ref[:, :n_xg], preferred_element_type=F32)
    qkvg_ref[0] = jnp.dot(h, w_ref[:, n_xg:], preferred_element_type=F32)


def _inproj_call(x, shift, scale, gain, w_in_bf, n_xg, per_batch_mod):
    b, t, d = x.shape
    n_all = w_in_bf.shape[1]
    tt = min(t, 512)
    mod_map = (lambda bi, i: (bi, 0, 0)) if per_batch_mod else (lambda bi, i: (0, 0, 0))
    return pl.pallas_call(
        functools.partial(_inproj_kernel, n_xg=n_xg),
        grid=(b, t // tt),
        in_specs=[
            pl.BlockSpec((1, tt, d), lambda bi, i: (bi, i, 0)),
            pl.BlockSpec((1, 1, d), mod_map),
            pl.BlockSpec((1, 1, d), mod_map),
            pl.BlockSpec((1, d), lambda bi, i: (0, 0)),
            pl.BlockSpec((d, n_all), lambda bi, i: (0, 0)),
        ],
        out_specs=[
            pl.BlockSpec((1, tt, n_xg), lambda bi, i: (bi, i, 0)),
            pl.BlockSpec((1, tt, n_all - n_xg), lambda bi, i: (bi, i, 0)),
        ],
        out_shape=[
            jax.ShapeDtypeStruct((b, t, n_xg), F32),
            jax.ShapeDtypeStruct((b, t, n_all - n_xg), F32),
        ],
        compiler_params=_cparams("parallel", "arbitrary"),
        name="inproj",
    )(x, shift, scale, gain, w_in_bf)


SCAN_ROWS = 64
LRU_CHUNK = 512


def _tile_scan(a, b, reverse):
    row = lax.broadcasted_iota(jnp.int32, a.shape, 0)
    for s in (1, 2, 4):
        if reverse:
            a_s = pltpu.roll(a, SUBLANES - s, 0)
            b_s = pltpu.roll(b, SUBLANES - s, 0)
            m = row < SUBLANES - s
        else:
            a_s = pltpu.roll(a, s, 0)
            b_s = pltpu.roll(b, s, 0)
            m = row >= s
        b = jnp.where(m, a * b_s + b, b)
        a = jnp.where(m, a * a_s, a)
    return a, b


def _lru_kernel(xa_ref, ga_ref, xac_ref, cw_ref, cb_ref, wg_ref, bg_ref, lam_ref, y_ref,
                hf_ref, u_ref, xe_ref, a_ref, b_ref, carry_ref, *, t_len, c_len, tc, nc):
    j = pl.program_id(1)
    half = wg_ref.shape[2]

    def conv(n):
        u = cb_ref[...]
        for k in range(cw_ref.shape[0]):
            u = u + cw_ref[k:k + 1, :] * xe_ref[6 + k:6 + k + n, :]
        return u

    def gates(u, n, dirn):
        lam = -lam_ref[dirn:dirn + 1, :]
        sp = jnp.maximum(lam, 0.0) + jnp.log1p(jnp.exp(-jnp.abs(lam)))
        for hf in range(2):
            uh = u[:, hf * half:(hf + 1) * half]
            z = jnp.dot(uh.astype(BF16), wg_ref[dirn, hf], preferred_element_type=F32) + bg_ref[dirn, hf]
            r = jax.nn.sigmoid(z[:, :half])
            i = jax.nn.sigmoid(z[:, half:])
            log_a = -LRU_C * r * sp[:, hf * half:(hf + 1) * half]
            a = jnp.exp(log_a)
            a_ref[0:n, hf * half:(hf + 1) * half] = a
            b_ref[0:n, hf * half:(hf + 1) * half] = jnp.sqrt((1.0 - a) * (1.0 + a)) * (i * uh)

    def scan(n, dirn, emit):
        ng = n // SCAN_ROWS
        nt = SCAN_ROWS // SUBLANES

        def body(gi, carry):
            g = gi if dirn == 0 else ng - 1 - gi
            base = pl.multiple_of(g * SCAN_ROWS, SCAN_ROWS)
            av = a_ref[pl.ds(base, SCAN_ROWS), :]
            bv = b_ref[pl.ds(base, SCAN_ROWS), :]
            hs = [None] * nt
            order = range(nt) if dirn == 0 else range(nt - 1, -1, -1)
            for k in order:
                a, b = _tile_scan(av[k * SUBLANES:(k + 1) * SUBLANES], bv[k * SUBLANES:(k + 1) * SUBLANES],
                                  dirn == 1)
                h = a * carry + b
                carry = h[SUBLANES - 1:SUBLANES] if dirn == 0 else h[0:1]
                hs[k] = h
            if emit is not None:
                emit(base, jnp.concatenate(hs, axis=0))
            return carry

        carry_ref[...] = lax.fori_loop(0, ng, body, carry_ref[...])

    def run(dirn):
        first = (j == 0) if dirn == 0 else (j == nc)
        ci = j if dirn == 0 else 2 * nc - 1 - j

        @pl.when(first)
        def _():
            zeros = jnp.zeros((SUBLANES, xe_ref.shape[1]), F32)
            xe_ref[0:SUBLANES] = zeros
            xe_ref[SUBLANES:SUBLANES + c_len] = xac_ref[0]
            xe_ref[SUBLANES + c_len:2 * SUBLANES + c_len] = zeros
            gates(conv(c_len), c_len, dirn)
            carry_ref[...] = jnp.zeros(carry_ref.shape, F32)
            scan(c_len, dirn, None)

        r0 = pl.multiple_of(ci * tc, tc)
        if dirn == 0:
            p0 = pl.multiple_of(jnp.maximum(r0 - SUBLANES, 0), SUBLANES)
            n0 = pl.multiple_of(jnp.minimum(r0 + tc, t_len - SUBLANES), SUBLANES)
            prev = xa_ref[0, pl.ds(p0, SUBLANES), :]
            nxt = xa_ref[0, pl.ds(n0, SUBLANES), :]
            xe_ref[0:SUBLANES] = jnp.where(ci > 0, prev, 0.0)
            xe_ref[SUBLANES:SUBLANES + tc] = xa_ref[0, pl.ds(r0, tc), :]
            xe_ref[SUBLANES + tc:2 * SUBLANES + tc] = jnp.where(ci < nc - 1, nxt, 0.0)
            u = conv(tc)
            u_ref[pl.ds(r0, tc), :] = u
        else:
            u = u_ref[pl.ds(r0, tc), :]
        gates(u, tc, dirn)

        if dirn == 0:
            def emit(base, h):
                hf_ref[pl.ds(r0 + base, SCAN_ROWS), :] = h
        else:
            def emit(base, h):
                tot = hf_ref[pl.ds(r0 + base, SCAN_ROWS), :] + h
                ga = ga_ref[0, pl.ds(base, SCAN_ROWS), :]
                y_ref[0, pl.ds(base, SCAN_ROWS), :] = (tot * jax.nn.gelu(ga)).astype(y_ref.dtype)
        scan(tc, dirn, emit)

    @pl.when(j < nc)
    def _():
        run(0)

    @pl.when(j >= nc)
    def _():
        run(1)


def _lru_call(xg, xg_c, conv_w, conv_b, wg, bg, lam):
    b, t, _ = xg.shape
    c_len = xg_c.shape[1]
    d_lru = conv_w.shape[1]
    tc = min(t, LRU_CHUNK)
    nc = t // tc
    assert c_len <= tc and c_len % SCAN_ROWS == 0 and tc % SCAN_ROWS == 0

    def chunk_map(bi, j):
        return (bi, jnp.where(j < nc, nc - 1, 2 * nc - 1 - j), 0)

    def ga_map(bi, j):
        return (bi, jnp.where(j < nc, nc - 1, 2 * nc - 1 - j), 1)

    return pl.pallas_call(
        functools.partial(_lru_kernel, t_len=t, c_len=c_len, tc=tc, nc=nc),
        grid=(b, 2 * nc),
        in_specs=[
            pl.BlockSpec((1, t, d_lru), lambda bi, j: (bi, 0, 0)),
            pl.BlockSpec((1, tc, d_lru), ga_map),
            pl.BlockSpec((1, c_len, d_lru), lambda bi, j: (bi, 0, 0)),
            pl.BlockSpec(conv_w.shape, lambda bi, j: (0, 0)),
            pl.BlockSpec((1, d_lru), lambda bi, j: (0, 0)),
            pl.BlockSpec(wg.shape, lambda bi, j: (0, 0, 0, 0)),
            pl.BlockSpec(bg.shape, lambda bi, j: (0, 0, 0, 0)),
            pl.BlockSpec(lam.shape, lambda bi, j: (0, 0)),
        ],
        out_specs=pl.BlockSpec((1, tc, d_lru), chunk_map),
        out_shape=jax.ShapeDtypeStruct((b, t, d_lru), BF16),
        scratch_shapes=[
            pltpu.VMEM((t, d_lru), F32),
            pltpu.VMEM((t, d_lru), F32),
            pltpu.VMEM((tc + 2 * SUBLANES, d_lru), F32),
            pltpu.VMEM((tc, d_lru), F32),
            pltpu.VMEM((tc, d_lru), F32),
            pltpu.VMEM((1, d_lru), F32),
        ],
        compiler_params=_cparams("parallel", "arbitrary"),
        name="lru",
    )(xg, xg, xg_c, conv_w, conv_b.reshape(1, d_lru), wg, bg, lam)


def _ret_kernel(q_ref, k_ref, v_ref, g_ref, kc_ref, vc_ref, cos_ref, sin_ref, dm_ref, rt_ref, cwt_ref, gc_ref,
                y_ref, qr_ref, kr_ref, sf_ref, sb_ref, *, nc, scale):
    C = RET_CHUNK
    dh = q_ref.shape[2]
    def rope(xv, cs, sn):
        return xv * cs + pltpu.roll(xv, dh // 2, 1) * sn

    xi_f, xi_b, ze_f, ze_b = rt_ref[0, 0], rt_ref[0, 1], rt_ref[0, 2], rt_ref[0, 3]
    g_chunk = gc_ref[0]
    g_full = jnp.concatenate([g_chunk] * (dh // SUBLANES), axis=0)

    kc = (kc_ref[0] * scale).T.astype(BF16)
    vc = vc_ref[0]
    s_f0 = jnp.dot(kc, (vc * cwt_ref[0, 0]).astype(BF16), preferred_element_type=F32)
    s_b0 = jnp.dot(kc, (vc * cwt_ref[0, 1]).astype(BF16), preferred_element_type=F32)

    def p1(n, s):
        r = pl.multiple_of(n * C, C)
        cs, sn = cos_ref[pl.ds(r, C), :], sin_ref[pl.ds(r, C), :]
        k = rope(k_ref[0, pl.ds(r, C), :], cs, sn) * scale
        qr_ref[pl.ds(r, C), :] = rope(q_ref[0, pl.ds(r, C), :], cs, sn)
        kr_ref[pl.ds(r, C), :] = k.astype(BF16)
        v = v_ref[0, pl.ds(r, C), :]
        vz = jnp.concatenate([v * ze_f, v * ze_b], axis=1).astype(BF16)
        kv = jnp.dot(k.T.astype(BF16), vz, preferred_element_type=F32)
        sf_ref[n] = s
        sb_ref[n] = kv[:, dh:]
        return g_full * s + kv[:, :dh]

    lax.fori_loop(0, nc, p1, s_f0, unroll=min(RET_UNROLL, nc))

    def p1b(i, s):
        n = nc - 1 - i
        kvb = sb_ref[n]
        sb_ref[n] = s
        return g_full * s + kvb

    lax.fori_loop(0, nc, p1b, s_b0)

    def p2(n, carry):
        r = pl.multiple_of(n * C, C)
        q = qr_ref[pl.ds(r, C), :]
        kb = kr_ref[pl.ds(r, C), :]
        vb = v_ref[0, pl.ds(r, C), :].astype(BF16)
        sc = lax.dot_general(q.astype(BF16), kb, NT_DIMS, preferred_element_type=F32) * dm_ref[0]
        o = jnp.dot(sc.astype(BF16), vb, preferred_element_type=F32)
        qx = jnp.concatenate([q * xi_f, q * xi_b], axis=1).astype(BF16)
        st = jnp.concatenate([sf_ref[n], sb_ref[n]], axis=0).astype(BF16)
        o = o + jnp.dot(qx, st, preferred_element_type=F32)
        o = o * lax.rsqrt(jnp.mean(o * o, axis=-1, keepdims=True) + EPS)
        y_ref[0, pl.ds(r, C), :] = (o * _silu(g_ref[0, pl.ds(r, C), :])).astype(y_ref.dtype)
        return carry

    lax.fori_loop(0, nc, p2, 0, unroll=min(RET_UNROLL, nc))


def _ret_tables(t, c_len, dh):
    C = RET_CHUNK
    hh = jnp.arange(RET_HEADS, dtype=F32)
    log_g = jnp.log1p(-jnp.exp2(-(5.0 + 2.0 * hh)))
    idx = jnp.arange(C, dtype=F32)
    dm = jnp.exp(jnp.abs(idx[:, None] - idx[None, :])[None] * log_g[:, None, None])

    def lanes(e):
        return jnp.broadcast_to(jnp.exp(e[None, :] * log_g[:, None])[:, :, None], (RET_HEADS, e.shape[0], dh))

    rt = jnp.stack([lanes(idx + 1.0), lanes(C - idx), lanes(C - 1.0 - idx), lanes(idx)], axis=1)
    pos = jnp.arange(c_len, dtype=F32)
    cwt = jnp.stack([lanes(c_len - 1.0 - pos), lanes(pos)], axis=1)
    gc = jnp.broadcast_to(jnp.exp(C * log_g)[:, None, None], (RET_HEADS, SUBLANES, dh))
    rows = t // GRID_W
    row = jnp.repeat(jnp.arange(rows, dtype=F32), GRID_W)
    col = jnp.tile(jnp.arange(GRID_W, dtype=F32), rows)
    q4 = dh // 4
    freqs = ROPE_BASE ** (-jnp.arange(q4, dtype=F32) / q4)
    ang_r = row[:, None] * freqs[None, :]
    ang_c = col[:, None] * freqs[None, :]
    cos = jnp.concatenate([jnp.cos(ang_r), jnp.cos(ang_c)] * 2, axis=1)
    sin = jnp.concatenate([-jnp.sin(ang_r), -jnp.sin(ang_c), jnp.sin(ang_r), jnp.sin(ang_c)], axis=1)
    return dm, rt, cwt, gc, cos, sin


def _pair_apart(w_in, n_xg):
    d, n_all = w_in.shape
    dh = (n_all - n_xg) // (4 * RET_HEADS)
    qk = w_in[:, n_xg:n_xg + 2 * RET_HEADS * dh].reshape(d, 2 * RET_HEADS, 2, 2, dh // 4)
    qk = jnp.swapaxes(qk, 2, 3).reshape(d, 2 * RET_HEADS * dh)
    return jnp.concatenate([w_in[:, :n_xg], qk, w_in[:, n_xg + 2 * RET_HEADS * dh:]], axis=1)


def _ret_call(qkvg, qkvg_c):
    b, t, w = qkvg.shape
    c_len = qkvg_c.shape[1]
    dh = w // (4 * RET_HEADS)
    nc = t // RET_CHUNK
    dm, rt, cwt, gc, cos, sin = _ret_tables(t, c_len, dh)
    H = RET_HEADS

    def col(off):
        return lambda bi, h: (bi, 0, off + h)

    return pl.pallas_call(
        functools.partial(_ret_kernel, nc=nc, scale=dh ** -0.5),
        grid=(b, H),
        in_specs=[
            pl.BlockSpec((1, t, dh), col(0)),
            pl.BlockSpec((1, t, dh), col(H)),
            pl.BlockSpec((1, t, dh), col(2 * H)),
            pl.BlockSpec((1, t, dh), col(3 * H)),
            pl.BlockSpec((1, c_len, dh), col(H)),
            pl.BlockSpec((1, c_len, dh), col(2 * H)),
            pl.BlockSpec((t, dh), lambda bi, h: (0, 0)),
            pl.BlockSpec((t, dh), lambda bi, h: (0, 0)),
            pl.BlockSpec((1, RET_CHUNK, RET_CHUNK), lambda bi, h: (h, 0, 0)),
            pl.BlockSpec((1, 4, RET_CHUNK, dh), lambda bi, h: (h, 0, 0, 0)),
            pl.BlockSpec((1, 2, c_len, dh), lambda bi, h: (h, 0, 0, 0)),
            pl.BlockSpec((1, SUBLANES, dh), lambda bi, h: (h, 0, 0)),
        ],
        out_specs=pl.BlockSpec((1, t, dh), col(0)),
        out_shape=jax.ShapeDtypeStruct((b, t, H * dh), BF16),
        scratch_shapes=[
            pltpu.VMEM((t, dh), F32),
            pltpu.VMEM((t, dh), BF16),
            pltpu.VMEM((nc, dh, dh), F32),
            pltpu.VMEM((nc, dh, dh), F32),
        ],
        compiler_params=_cparams("parallel", "arbitrary"),
        name="retention",
    )(qkvg, qkvg, qkvg, qkvg, qkvg_c, qkvg_c, cos, sin, dm, rt, cwt, gc)


def _route(logits_t, bias):
    e, n = logits_t.shape
    per = e // N_GROUPS
    s = jax.nn.sigmoid(logits_t)
    sb = s + bias
    neg = -jnp.inf
    g3 = sb.reshape(N_GROUPS, per, n)
    it3 = lax.broadcasted_iota(jnp.int32, g3.shape, 1)
    m1 = jnp.max(g3, axis=1, keepdims=True)
    i1 = jnp.min(jnp.where(g3 == m1, it3, per), axis=1, keepdims=True)
    m2 = jnp.max(jnp.where(it3 == i1, neg, g3), axis=1, keepdims=True)
    gs = (m1 + m2).reshape(N_GROUPS, n)
    itg = lax.broadcasted_iota(jnp.int32, gs.shape, 0)
    gsel = jnp.zeros(gs.shape, jnp.bool_)
    for _ in range(TOP_GROUPS):
        m = jnp.max(gs, axis=0, keepdims=True)
        i = jnp.min(jnp.where(gs == m, itg, N_GROUPS), axis=0, keepdims=True)
        hit = itg == i
        gsel = gsel | hit
        gs = jnp.where(hit, neg, gs)
    gmask = jnp.broadcast_to(gsel.reshape(N_GROUPS, 1, n), (N_GROUPS, per, n)).reshape(e, n)
    v = jnp.where(gmask, sb, neg)
    ite = lax.broadcasted_iota(jnp.int32, v.shape, 0)
    sel = jnp.zeros(v.shape, jnp.bool_)
    for _ in range(TOP_K):
        m = jnp.max(v, axis=0, keepdims=True)
        i = jnp.min(jnp.where(v == m, ite, e), axis=0, keepdims=True)
        hit = ite == i
        sel = sel | hit
        v = jnp.where(hit, neg, v)
    picked = jnp.where(sel, s, 0.0)
    return picked / jnp.sum(picked, axis=0, keepdims=True) * ROUTE_SCALE


def _outproj_kernel(yl_ref, yr_ref, x_ref, g1_ref, sh_ref, sc_ref, nf_ref, wo_ref, wr_ref, rb_ref,
                    x1_ref, h2_ref, gt_ref, cnt_ref):
    dl = yl_ref.shape[2]
    ts = gt_ref.shape[1] // cnt_ref.shape[0]
    w = wr_ref[...]
    w_hi = w.astype(BF16)
    r1 = w - w_hi.astype(F32)
    w_mid = r1.astype(BF16)
    w_lo = (r1 - w_mid.astype(F32)).astype(BF16)
    for k in range(cnt_ref.shape[0]):
        rows = pl.ds(k * ts, ts)
        mx = jnp.dot(yl_ref[0, rows, :], wo_ref[0:dl, :], preferred_element_type=F32)
        mx = mx + jnp.dot(yr_ref[0, rows, :], wo_ref[dl:, :], preferred_element_type=F32)
        x1 = x_ref[0, rows, :] + g1_ref[0] * mx
        x1_ref[0, rows, :] = x1
        h = _rms_mod(x1, nf_ref[...], sh_ref[0], sc_ref[0])
        h_hi = h.astype(BF16)
        h2_ref[0, rows, :] = h_hi
        h_lo = (h - h_hi.astype(F32)).astype(BF16)
        small = (_dot32(h_hi, w_lo) + _dot32(h_lo, w_mid)) + (_dot32(h_hi, w_mid) + _dot32(h_lo, w_hi))
        logits = small + _dot32(h_hi, w_hi)
        ne = logits.shape[1]
        logits_t = jnp.concatenate([logits, jnp.zeros((ts, LANES - ne), F32)], axis=1).T[:ne, :]
        gates = _route(logits_t, rb_ref[...])
        gt_ref[:, rows] = gates
        hit = (gates > 0.0).astype(BF16)
        cnt_ref[k] = lax.dot_general(jnp.ones((SUBLANES, ts), BF16), hit, NT_DIMS, preferred_element_type=F32)


def _outproj_call(y_lru, y_ret, x, g1, sh2, sc2, norm_ffn, w_out_bf, w_router, b_router):
    b, t, d = x.shape
    dl = y_lru.shape[2]
    e = w_router.shape[1]
    ts = min(t, MOE_TILE)
    tt = min(t, OUTPROJ_TILES * ts)
    nt = t // tt
    per = tt // ts
    mod_map = lambda bi, i: (bi, 0, 0)
    return pl.pallas_call(
        _outproj_kernel,
        grid=(b, nt),
        in_specs=[
            pl.BlockSpec((1, tt, dl), lambda bi, i: (bi, i, 0)),
            pl.BlockSpec((1, tt, d - dl), lambda bi, i: (bi, i, 0)),
            pl.BlockSpec((1, tt, d), lambda bi, i: (bi, i, 0)),
            pl.BlockSpec((1, 1, d), mod_map),
            pl.BlockSpec((1, 1, d), mod_map),
            pl.BlockSpec((1, 1, d), mod_map),
            pl.BlockSpec((1, d), lambda bi, i: (0, 0)),
            pl.BlockSpec((d, d), lambda bi, i: (0, 0)),
            pl.BlockSpec((d, e), lambda bi, i: (0, 0)),
            pl.BlockSpec((e, 1), lambda bi, i: (0, 0)),
        ],
        out_specs=[
            pl.BlockSpec((1, tt, d), lambda bi, i: (bi, i, 0)),
            pl.BlockSpec((1, tt, d), lambda bi, i: (bi, i, 0)),
            pl.BlockSpec((e, tt), lambda bi, i: (0, bi * nt + i)),
            pl.BlockSpec((per, SUBLANES, e), lambda bi, i: (bi * nt + i, 0, 0)),
        ],
        out_shape=[
            jax.ShapeDtypeStruct((b, t, d), F32),
            jax.ShapeDtypeStruct((b, t, d), BF16),
            jax.ShapeDtypeStruct((e, b * t), F32),
            jax.ShapeDtypeStruct((b * nt * per, SUBLANES, e), F32),
        ],
        compiler_params=_cparams("parallel", "arbitrary"),
        name="outproj_route",
    )(y_lru, y_ret, x, g1, sh2, sc2, norm_ffn.reshape(1, d), w_out_bf, w_router, b_router.reshape(e, 1))


MOE_TILE = 256
MOE_CHUNK = 16
MOE_RB = 512
MOE_ROWS_COMMON = 2048
MOE_BM = 2048
MOE_EXPERT_ROWS = 1024
OUTPROJ_TILES = 2
NO_RANK = 512.0

TN_DIMS = (((0,), (0,)), ((), ()))
TAB_LOFF, TAB_GBASE, TAB_CPAD, TAB_NROWS, TAB_EEND = range(5)


def _round_up(v, m):
    return -(-v // m) * m


def _tables_kernel(cnt_ref, loff_ref, gbase_ref, cpad_ref, eend_ref, blk_ref, nused_ref, ctab_ref, *, nb):
    cnt = cnt_ref[...]
    ns, ne = cnt.shape
    cpad = jnp.ceil(cnt * (1.0 / MOE_CHUNK)) * MOE_CHUNK
    upper = (lax.broadcasted_iota(jnp.int32, (ne, ne), 0) < lax.broadcasted_iota(jnp.int32, (ne, ne), 1)).astype(F32)
    lower = (lax.broadcasted_iota(jnp.int32, (ns, ns), 1) < lax.broadcasted_iota(jnp.int32, (ns, ns), 0)).astype(F32)
    loff = jnp.dot(cpad, upper, precision=HIGHEST, preferred_element_type=F32)
    before = jnp.dot(lower, cpad, precision=HIGHEST, preferred_element_type=F32)
    tot = jnp.sum(cpad, axis=0, keepdims=True)
    epad = jnp.ceil(tot * (1.0 / MOE_BM)) * MOE_BM
    epad8 = jnp.broadcast_to(epad, (SUBLANES, ne))
    ebase = jnp.dot(epad8, upper, precision=HIGHEST, preferred_element_type=F32)[0:1]
    eend = ebase + epad
    loff_ref[...] = loff
    gbase_ref[...] = ebase + before
    cpad_ref[...] = cpad
    eend_ref[...] = eend
    start = lax.broadcasted_iota(jnp.int32, (nb, ne), 0).astype(F32) * MOE_BM
    blk = jnp.sum((eend <= start).astype(F32), axis=1, keepdims=True)
    blk_ref[...] = jnp.minimum(blk, ne - 1.0)
    nused_ref[...] = jnp.sum(epad, axis=1, keepdims=True) * (1.0 / MOE_BM)
    nch = ctab_ref.shape[1]
    cidx = lax.broadcasted_iota(jnp.int32, (ns, nch), 1).astype(F32)
    shift = (ebase + before - loff) * (1.0 / MOE_CHUNK)
    ctab = jnp.full((ns, nch), -1.0, F32)
    for e in range(ne):
        lo = loff[:, e:e + 1] * (1.0 / MOE_CHUNK)
        hi = lo + cpad[:, e:e + 1] * (1.0 / MOE_CHUNK)
        ctab = jnp.where((cidx >= lo) & (cidx < hi), cidx + shift[:, e:e + 1], ctab)
    ctab_ref[...] = ctab


def _tables_call(cnt, nb, nch):
    ns, ne = cnt.shape
    f = lambda shape: jax.ShapeDtypeStruct(shape, F32)
    return pl.pallas_call(
        functools.partial(_tables_kernel, nb=nb),
        out_shape=[f((ns, ne)), f((ns, ne)), f((ns, ne)), f((1, ne)), f((nb, 1)), f((1, 1)), f((ns, nch))],
        compiler_params=pltpu.CompilerParams(vmem_limit_bytes=VMEM_LIMIT_BYTES),
        name="moe_tables",
    )(cnt)


def _tile_rank_rhs(gt, lcol):
    ne, ts = gt.shape
    hit = gt > 0.0
    before = (lax.broadcasted_iota(jnp.int32, (ts, ts), 0) < lax.broadcasted_iota(jnp.int32, (ts, ts), 1)).astype(BF16)
    rank = jnp.dot(hit.astype(BF16), before, preferred_element_type=F32)
    rm = jnp.where(hit, rank, NO_RANK)
    q = jnp.broadcast_to(lcol * (1.0 / MOE_CHUNK), (ne, LANES))
    return jnp.concatenate([rm, q], axis=1).astype(BF16)


def _onehot_rows(r0, rhs, lrow, crow, ts):
    ne = lrow.shape[1]
    rows = (r0 + lax.broadcasted_iota(jnp.int32, (MOE_RB, ne), 0)).astype(F32)
    seg = ((rows >= lrow) & (rows < crow)).astype(BF16)
    ex = jnp.dot(seg, rhs, preferred_element_type=F32)
    rowl = (r0 + lax.broadcasted_iota(jnp.int32, (MOE_RB, LANES), 0)).astype(F32)
    tgt = rowl - MOE_CHUNK * ex[:, ts:ts + LANES]
    p = ex[:, :ts] == jnp.concatenate([tgt] * (ts // LANES), axis=1)
    return p, (ex[:, ts + LANES:] if rhs.shape[1] > ts + LANES else None)


def _chunk_copy(buf, slot, c, row, hbm, sem, to_hbm):
    loc = buf.at[slot, pl.ds(c * MOE_CHUNK, MOE_CHUNK)]
    glob = hbm.at[pl.ds(pl.multiple_of(row, MOE_CHUNK), MOE_CHUNK)]
    return pltpu.make_async_copy(loc, glob, sem.at[slot]) if to_hbm else pltpu.make_async_copy(glob, loc, sem.at[slot])


def _wait_chunks(n, buf, slot, hbm, sem, to_hbm):
    for _ in range(n):
        _chunk_copy(buf, slot, 0, 0, hbm, sem, to_hbm).wait()


def _wait_chunks_dyn(n, buf, slot, hbm, sem, to_hbm):
    def body(i, c):
        _chunk_copy(buf, slot, 0, 0, hbm, sem, to_hbm).wait()
        return c

    lax.fori_loop(0, n, body, 0)


def _dispatch_kernel(tab_ref, ctab_ref, h_ref, gt_ref, lrow_ref, crow_ref, lcol_ref, xs_hbm, xbuf, zbuf, sem, tailf,
                     *, ns, trash0):
    s = pl.program_id(0)
    slot = s % 2
    ts = h_ref.shape[0]
    ne = gt_ref.shape[0]
    rows_max = xbuf.shape[1]
    nch = rows_max // MOE_CHUNK
    rows_c = min(rows_max, MOE_ROWS_COMMON)
    nch_c = rows_c // MOE_CHUNK

    def wait_slot(sl):
        _wait_chunks(nch_c, xbuf, sl, xs_hbm, sem, True)
        if nch > nch_c:
            @pl.when(tailf[sl] == 1)
            def _():
                _wait_chunks(nch - nch_c, xbuf, sl, xs_hbm, sem, True)

    @pl.when(s >= 2)
    def _():
        wait_slot(slot)

    rhs = _tile_rank_rhs(gt_ref[...], lcol_ref[0])

    def rows_block(r0):
        p, _ = _onehot_rows(r0, rhs, lrow_ref[0], crow_ref[0], ts)
        xbuf[slot, r0:r0 + MOE_RB, :] = jnp.dot(p.astype(BF16), h_ref[...],
                                                preferred_element_type=F32).astype(xbuf.dtype)

    def copy_chunks(c0, c1):
        for c in range(c0, c1):
            g = ctab_ref[0, 0, c]
            row = jnp.where(g >= 0, g * MOE_CHUNK, trash0 + slot * rows_max + c * MOE_CHUNK)
            _chunk_copy(xbuf, slot, c, row, xs_hbm, sem, True).start()

    for r0 in range(0, rows_c, MOE_RB):
        rows_block(r0)
    copy_chunks(0, nch_c)
    if nch > nch_c:
        long_tile = ctab_ref[0, 0, nch_c] >= 0
        tailf[slot] = long_tile.astype(jnp.int32)

        @pl.when(long_tile)
        def _():
            for r0 in range(rows_c, rows_max, MOE_RB):
                rows_block(r0)
            copy_chunks(nch_c, nch)

    @pl.when(s == ns - 1)
    def _():
        zbuf[...] = jnp.zeros(zbuf.shape, zbuf.dtype)

        def per_expert(e, total):
            first = tab_ref[0, TAB_GBASE, e] + tab_ref[0, TAB_CPAD, e]
            ntail = lax.shift_right_logical(tab_ref[0, TAB_EEND, e] - first, MOE_CHUNK.bit_length() - 1)

            def per_chunk(j, c2):
                dst = xs_hbm.at[pl.ds(pl.multiple_of(first + j * MOE_CHUNK, MOE_CHUNK), MOE_CHUNK)]
                pltpu.make_async_copy(zbuf, dst, sem.at[slot]).start()
                return c2

            lax.fori_loop(0, ntail, per_chunk, 0)
            return total + ntail

        ntail = lax.fori_loop(0, ne, per_expert, 0)
        _wait_chunks_dyn(ntail, xbuf, slot, xs_hbm, sem, True)
        wait_slot(slot)
        if ns >= 2:
            wait_slot(1 - slot)


def _dispatch_call(tab, ctab, h2, gt, lrow, crow, lcol, p_max, rows_max):
    n, d = h2.shape
    ne = gt.shape[0]
    ts = min(n, MOE_TILE)
    ns = n // ts
    nch = rows_max // MOE_CHUNK
    return pl.pallas_call(
        functools.partial(_dispatch_kernel, ns=ns, trash0=p_max),
        grid=(ns,),
        in_specs=[
            pl.BlockSpec((1, 5, ne), lambda s: (s, 0, 0), memory_space=pltpu.SMEM),
            pl.BlockSpec((1, 1, nch), lambda s: (s, 0, 0), memory_space=pltpu.SMEM),
            pl.BlockSpec((ts, d), lambda s: (s, 0)),
            pl.BlockSpec((ne, ts), lambda s: (0, s)),
            pl.BlockSpec((1, 1, ne), lambda s: (s, 0, 0)),
            pl.BlockSpec((1, 1, ne), lambda s: (s, 0, 0)),
            pl.BlockSpec((1, ne, 1), lambda s: (s, 0, 0)),
        ],
        out_specs=pl.BlockSpec(memory_space=pl.ANY),
        out_shape=jax.ShapeDtypeStruct((p_max + 2 * rows_max, d), BF16),
        scratch_shapes=[
            pltpu.VMEM((2, rows_max, d), BF16),
            pltpu.VMEM((MOE_CHUNK, d), BF16),
            pltpu.SemaphoreType.DMA((2,)),
            pltpu.SMEM((2,), jnp.int32),
        ],
        compiler_params=_cparams("arbitrary"),
        name="moe_dispatch",
    )(tab, ctab, h2, gt, lrow, crow, lcol)


def _experts_kernel(blk_ref, nused_ref, x_ref, w1_ref, w3_ref, w2_ref, y_ref):
    del blk_ref

    @pl.when(pl.program_id(0) < nused_ref[0])
    def _():
        for r0 in range(0, x_ref.shape[0], MOE_EXPERT_ROWS):
            xv = x_ref[r0:r0 + MOE_EXPERT_ROWS, :]
            hid = _silu(_dot32(xv, w1_ref[0])) * _dot32(xv, w3_ref[0])
            y_ref[r0:r0 + MOE_EXPERT_ROWS, :] = _dot32(hid.astype(BF16), w2_ref[0]).astype(y_ref.dtype)


def _experts_call(blk, nused, xs, w1, w3, w2):
    d = xs.shape[1]
    _, _, de = w1.shape
    nb = blk.shape[0]
    p_max = nb * MOE_BM
    row_map = lambda i, blk_r, nu_r: (jnp.maximum(jnp.minimum(i, nu_r[0] - 1), 0), 0)
    w_map = lambda i, blk_r, nu_r: (blk_r[i], 0, 0)
    return pl.pallas_call(
        _experts_kernel,
        grid_spec=pltpu.PrefetchScalarGridSpec(
            num_scalar_prefetch=2,
            grid=(nb,),
            in_specs=[
                pl.BlockSpec((MOE_BM, d), row_map),
                pl.BlockSpec((1, d, de), w_map),
                pl.BlockSpec((1, d, de), w_map),
                pl.BlockSpec((1, de, d), w_map),
            ],
            out_specs=pl.BlockSpec((MOE_BM, d), row_map),
        ),
        out_shape=jax.ShapeDtypeStruct((p_max, d), BF16),
        compiler_params=_cparams("arbitrary"),
        name="moe_experts",
    )(blk, nused, xs, w1, w3, w2)


def _combine_kernel(ctab_ref, ctabn_ref, gt_ref, lrow_ref, crow_ref, lcol_ref, h_ref, x1_ref, g2_ref,
                    s1_ref, s3_ref, s2_ref, nfin_ref, ys_hbm, o_ref, ybuf, acc_ref, sem, *, ns):
    s = pl.program_id(0)
    slot = s % 2
    ts = h_ref.shape[0]
    rows_max = ybuf.shape[1]
    nch = rows_max // MOE_CHUNK
    rows_c = min(rows_max, MOE_ROWS_COMMON)
    nch_c = rows_c // MOE_CHUNK

    def chunk_range(tab, sl, c0, c1):
        for c in range(c0, c1):
            row = jnp.maximum(tab[0, 0, c], 0) * MOE_CHUNK
            _chunk_copy(ybuf, sl, c, row, ys_hbm, sem, False).start()

    def gather(tab, sl):
        chunk_range(tab, sl, 0, nch_c)
        if nch > nch_c:
            @pl.when(tab[0, 0, nch_c] >= 0)
            def _():
                chunk_range(tab, sl, nch_c, nch)

    def wait(tab, sl):
        _wait_chunks(nch_c, ybuf, sl, ys_hbm, sem, False)
        if nch > nch_c:
            @pl.when(tab[0, 0, nch_c] >= 0)
            def _():
                _wait_chunks(nch - nch_c, ybuf, sl, ys_hbm, sem, False)

    @pl.when(s == 0)
    def _():
        gather(ctab_ref, 0)

    gather(ctabn_ref, 1 - slot)

    h = h_ref[...]
    hs = _silu(jnp.dot(h, s1_ref[...], preferred_element_type=F32)) * jnp.dot(h, s3_ref[...], preferred_element_type=F32)
    acc = jnp.dot(hs.astype(BF16), s2_ref[...], preferred_element_type=F32)

    gt = gt_ref[...]
    rhs = jnp.concatenate([_tile_rank_rhs(gt, lcol_ref[0]), gt.astype(BF16)], axis=1)
    wait(ctab_ref, slot)

    def rows_block(r0):
        p, gexp = _onehot_rows(r0, rhs, lrow_ref[0], crow_ref[0], ts)
        gm = jnp.where(p, gexp, 0.0).astype(BF16)
        return lax.dot_general(gm, ybuf[slot, r0:r0 + MOE_RB, :], TN_DIMS, preferred_element_type=F32)

    for r0 in range(0, rows_c, MOE_RB):
        acc = acc + rows_block(r0)
    if nch > nch_c:
        acc_ref[...] = acc

        @pl.when(ctab_ref[0, 0, nch_c] >= 0)
        def _():
            for r0 in range(rows_c, rows_max, MOE_RB):
                acc_ref[...] += rows_block(r0)

        acc = acc_ref[...]
    xo = x1_ref[...] + g2_ref[0] * acc
    o_ref[...] = xo * lax.rsqrt(jnp.mean(xo * xo, axis=-1, keepdims=True) + EPS) * nfin_ref[...]

    @pl.when(s == ns - 1)
    def _():
        wait(ctabn_ref, 1 - slot)


def _combine_call(ctab, gt, lrow, crow, lcol, h2, x1, g2, ws1, ws3, ws2, norm_final, ys, rows_max, t):
    n, d = h2.shape
    ne = gt.shape[0]
    ts = min(n, MOE_TILE)
    ns = n // ts
    per_b = t // ts
    nch = rows_max // MOE_CHUNK
    return pl.pallas_call(
        functools.partial(_combine_kernel, ns=ns),
        grid=(ns,),
        in_specs=[
            pl.BlockSpec((1, 1, nch), lambda s: (s, 0, 0), memory_space=pltpu.SMEM),
            pl.BlockSpec((1, 1, nch), lambda s: (jnp.minimum(s + 1, ns - 1), 0, 0), memory_space=pltpu.SMEM),
            pl.BlockSpec((ne, ts), lambda s: (0, s)),
            pl.BlockSpec((1, 1, ne), lambda s: (s, 0, 0)),
            pl.BlockSpec((1, 1, ne), lambda s: (s, 0, 0)),
            pl.BlockSpec((1, ne, 1), lambda s: (s, 0, 0)),
            pl.BlockSpec((ts, d), lambda s: (s, 0)),
            pl.BlockSpec((ts, d), lambda s: (s, 0)),
            pl.BlockSpec((1, 1, d), lambda s: (s // per_b, 0, 0)),
            pl.BlockSpec(ws1.shape, lambda s: (0, 0)),
            pl.BlockSpec(ws3.shape, lambda s: (0, 0)),
            pl.BlockSpec(ws2.shape, lambda s: (0, 0)),
            pl.BlockSpec((1, d), lambda s: (0, 0)),
            pl.BlockSpec(memory_space=pl.ANY),
        ],
        out_specs=pl.BlockSpec((ts, d), lambda s: (s, 0)),
        out_shape=jax.ShapeDtypeStruct((n, d), F32),
        scratch_shapes=[
            pltpu.VMEM((2, rows_max, d), BF16),
            pltpu.VMEM((ts, d), F32),
            pltpu.SemaphoreType.DMA((2,)),
        ],
        compiler_params=_cparams("arbitrary"),
        name="moe_combine",
    )(ctab, ctab, gt, lrow, crow, lcol, h2, x1, g2, ws1, ws3, ws2, norm_final.reshape(1, d), ys)


def _moe_call(h2, gt, cnt, x1, g2, w1, w3, w2, ws1, ws3, ws2, norm_final, t):
    n, d = h2.shape
    ne = gt.shape[0]
    ts = min(n, MOE_TILE)
    ns = n // ts
    rows_max = _round_up(TOP_K * ts + ne * (MOE_CHUNK - 1), MOE_RB)
    p_max = _round_up(n * TOP_K + ns * ne * (MOE_CHUNK - 1) + ne * (MOE_BM - 1), MOE_BM)
    nb = p_max // MOE_BM

    nch = rows_max // MOE_CHUNK
    loff, gbase, cpad, eend, blk, nused, ctab = _tables_call(cnt[:, 0, :], nb, nch)
    nrows = loff[:, ne - 1:] + cpad[:, ne - 1:]
    tab = jnp.stack([loff, gbase, cpad, jnp.broadcast_to(nrows, (ns, ne)), jnp.broadcast_to(eend, (ns, ne))],
                    axis=1).astype(jnp.int32)
    ctab = ctab.astype(jnp.int32).reshape(ns, 1, nch)
    lrow = loff.reshape(ns, 1, ne)
    crow = (loff + cpad).reshape(ns, 1, ne)
    lcol = loff.reshape(ns, ne, 1)

    xs = _dispatch_call(tab, ctab, h2, gt, lrow, crow, lcol, p_max, rows_max)
    ys = _experts_call(blk.reshape(nb).astype(jnp.int32), nused.reshape(1).astype(jnp.int32), xs, w1, w3, w2)
    return _combine_call(ctab, gt, lrow, crow, lcol, h2, x1, g2, ws1, ws3, ws2, norm_final, ys, rows_max, t)


def _block_diag(w):
    nb, n, _ = w.shape
    eye = jnp.eye(nb, dtype=w.dtype)
    return (eye[:, None, :, None] * w[:, :, None, :]).reshape(nb * n, nb * n)


def _lru_gate_weights(w_rec, b_rec, w_inp, b_inp):
    d_lru = b_rec.shape[1]
    half = d_lru // 2
    ws, bs = [], []
    for dirn in range(2):
        wr, wi = _block_diag(w_rec[dirn]), _block_diag(w_inp[dirn])
        ws.append(jnp.stack([jnp.concatenate([wr[s:s + half, s:s + half], wi[s:s + half, s:s + half]], axis=1)
                             for s in (0, half)]))
        bs.append(jnp.stack([jnp.concatenate([b_rec[dirn, s:s + half], b_inp[dirn, s:s + half]])[None, :]
                             for s in (0, half)]))
    return jnp.stack(ws).astype(BF16), jnp.stack(bs).astype(F32)


def kernel(x, c, ctx, c_ctx, w_mod, b_mod, norm_mix, norm_ffn, w_in, conv_w, conv_b, lru_w_rec, lru_b_rec, lru_w_in, lru_b_in, lru_lambda, w_out, w_router, b_router, w_exp_gate, w_exp_up, w_exp_down, w_sh_gate, w_sh_up, w_sh_down, norm_final):
    b, t, d = x.shape
    depth = w_mod.shape[0]
    assert depth == 1, "context-stream update for deeper stacks is not implemented"
    l = 0
    d_lru = conv_w.shape[2]
    n_xg = 2 * d_lru

    rows = -(-(b + 1) // SUBLANES) * SUBLANES
    cc = jnp.concatenate([c, c_ctx[None, :], jnp.zeros((rows - b - 1, d), F32)], axis=0)
    mod = _mod_call(cc, w_mod[l], b_mod[l]).reshape(rows, 1, -1)
    sh1, sc1, g1, sh2, sc2, g2 = (mod[:, :, i * d:(i + 1) * d] for i in range(6))

    w_in_bf = _pair_apart(w_in[l], n_xg).astype(BF16)
    gain_mix = norm_mix[l].reshape(1, d)
    xg, qkvg = _inproj_call(x, sh1[:b], sc1[:b], gain_mix, w_in_bf, n_xg, True)
    xg_c, qkvg_c = _inproj_call(ctx, sh1[b:b + 1], sc1[b:b + 1], gain_mix, w_in_bf, n_xg, False)

    wg, bg = _lru_gate_weights(lru_w_rec[l], lru_b_rec[l], lru_w_in[l], lru_b_in[l])
    y_lru = _lru_call(xg, xg_c, conv_w[l], conv_b[l], wg, bg, lru_lambda[l])
    y_ret = _ret_call(qkvg, qkvg_c)

    x1, h2, gt, cnt = _outproj_call(y_lru, y_ret, x, g1[:b], sh2[:b], sc2[:b], norm_ffn[l], w_out[l].astype(BF16),
                                    w_router[l], b_router[l])

    out = _moe_call(h2.reshape(b * t, d), gt, cnt, x1.reshape(b * t, d), g2[:b],
                    w_exp_gate[l].astype(BF16), w_exp_up[l].astype(BF16), w_exp_down[l].astype(BF16),
                    w_sh_gate[l].astype(BF16), w_sh_up[l].astype(BF16), w_sh_down[l].astype(BF16), norm_final, t)
    return out.reshape(b, t, d)
```

```python
import functools

import jax
import jax.numpy as jnp
from jax import lax
from jax.experimental import pallas as pl
from jax.experimental.pallas import tpu as pltpu

F32 = jnp.float32
BF16 = jnp.bfloat16
HIGHEST = lax.Precision.HIGHEST

EPS = 1e-6
GRID_W = 64
LRU_C = 8.0
RET_HEADS = 4
RET_CHUNK = 128
RET_UNROLL = 32
ROPE_BASE = 10000.0
N_GROUPS = 8
TOP_GROUPS = 4
TOP_K = 6
ROUTE_SCALE = 2.5

LANES = 128
SUBLANES = 8
VMEM_LIMIT_BYTES = 56 * 1024 * 1024

NT_DIMS = (((1,), (1,)), ((), ()))


def _cparams(*sem):
    return pltpu.CompilerParams(dimension_semantics=sem, vmem_limit_bytes=VMEM_LIMIT_BYTES)


def _silu(v):
    return v * jax.nn.sigmoid(v)


def _dot32(a, b):
    return jnp.dot(a, b, preferred_element_type=F32)


def _rms_mod(x, gain, shift, scale):
    y = x * lax.rsqrt(jnp.mean(x * x, axis=-1, keepdims=True) + EPS) * gain
    return y * (1.0 + scale) + shift


def _mod_kernel(c_ref, w_ref, b_ref, o_ref):
    s = _silu(c_ref[...])
    o_ref[...] = jnp.dot(s, w_ref[...], precision=HIGHEST, preferred_element_type=F32) + b_ref[...]


def _mod_call(cc, w_mod, b_mod):
    rows, d = cc.shape
    n = w_mod.shape[1]
    tn = 512
    return pl.pallas_call(
        _mod_kernel,
        grid=(n // tn,),
        in_specs=[
            pl.BlockSpec((rows, d), lambda j: (0, 0)),
            pl.BlockSpec((d, tn), lambda j: (0, j)),
            pl.BlockSpec((1, tn), lambda j: (0, j)),
        ],
        out_specs=pl.BlockSpec((rows, tn), lambda j: (0, j)),
        out_shape=jax.ShapeDtypeStruct((rows, n), F32),
        compiler_params=_cparams("arbitrary"),
        name="mod",
    )(cc, w_mod, b_mod.reshape(1, n))


def _inproj_kernel(x_ref, sh_ref, sc_ref, g_ref, w_ref, xg_ref, qkvg_ref, *, n_xg):
    h = _rms_mod(x_ref[0], g_ref[...], sh_ref[0], sc_ref[0]).astype(BF16)
    xg_ref[0] = jnp.dot(h, w_ref[:, :n_xg], preferred_element_type=F32)
    qkvg_ref[0] = jnp.dot(h, w_ref[:, n_xg:], preferred_element_type=F32)


def _inproj_call(x, shift, scale, gain, w_in_bf, n_xg, per_batch_mod):
    b, t, d = x.shape
    n_all = w_in_bf.shape[1]
    tt = min(t, 512)
    mod_map = (lambda bi, i: (bi, 0, 0)) if per_batch_mod else (lambda bi, i: (0, 0, 0))
    return pl.pallas_call(
        functools.partial(_inproj_kernel, n_xg=n_xg),
        grid=(b, t // tt),
        in_specs=[
            pl.BlockSpec((1, tt, d), lambda bi, i: (bi, i, 0)),
            pl.BlockSpec((1, 1, d), mod_map),
            pl.BlockSpec((1, 1, d), mod_map),
            pl.BlockSpec((1, d), lambda bi, i: (0, 0)),
            pl.BlockSpec((d, n_all), lambda bi, i: (0, 0)),
        ],
        out_specs=[
            pl.BlockSpec((1, tt, n_xg), lambda bi, i: (bi, i, 0)),
            pl.BlockSpec((1, tt, n_all - n_xg), lambda bi, i: (bi, i, 0)),
        ],
        out_shape=[
            jax.ShapeDtypeStruct((b, t, n_xg), F32),
            jax.ShapeDtypeStruct((b, t, n_all - n_xg), F32),
        ],
        compiler_params=_cparams("parallel", "arbitrary"),
        name="inproj",
    )(x, shift, scale, gain, w_in_bf)


SCAN_ROWS = 64
LRU_CHUNK = 512


def _tile_scan(a, b, reverse):
    row = lax.broadcasted_iota(jnp.int32, a.shape, 0)
    for s in (1, 2, 4):
        if reverse:
            a_s = pltpu.roll(a, SUBLANES - s, 0)
            b_s = pltpu.roll(b, SUBLANES - s, 0)
            m = row < SUBLANES - s
        else:
            a_s = pltpu.roll(a, s, 0)
            b_s = pltpu.roll(b, s, 0)
            m = row >= s
        b = jnp.where(m, a * b_s + b, b)
        a = jnp.where(m, a * a_s, a)
    return a, b


def _lru_kernel(xa_ref, ga_ref, xac_ref, cw_ref, cb_ref, wg_ref, bg_ref, lam_ref, y_ref,
                hf_ref, u_ref, xe_ref, a_ref, b_ref, carry_ref, *, t_len, c_len, tc, nc):
    j = pl.program_id(1)
    half = wg_ref.shape[2]

    def conv(n):
        u = cb_ref[...]
        for k in range(cw_ref.shape[0]):
            u = u + cw_ref[k:k + 1, :] * xe_ref[6 + k:6 + k + n, :]
        return u

    def gates(u, n, dirn):
        lam = -lam_ref[dirn:dirn + 1, :]
        sp = jnp.maximum(lam, 0.0) + jnp.log1p(jnp.exp(-jnp.abs(lam)))
        for hf in range(2):
            uh = u[:, hf * half:(hf + 1) * half]
            z = jnp.dot(uh.astype(BF16), wg_ref[dirn, hf], preferred_element_type=F32) + bg_ref[dirn, hf]
            r = jax.nn.sigmoid(z[:, :half])
            i = jax.nn.sigmoid(z[:, half:])
            log_a = -LRU_C * r * sp[:, hf * half:(hf + 1) * half]
            a = jnp.exp(log_a)
            a_ref[0:n, hf * half:(hf + 1) * half] = a
            b_ref[0:n, hf * half:(hf + 1) * half] = jnp.sqrt((1.0 - a) * (1.0 + a)) * (i * uh)

    def scan(n, dirn, emit):
        ng = n // SCAN_ROWS
        nt = SCAN_ROWS // SUBLANES

        def body(gi, carry):
            g = gi if dirn == 0 else ng - 1 - gi
            base = pl.multiple_of(g * SCAN_ROWS, SCAN_ROWS)
            av = a_ref[pl.ds(base, SCAN_ROWS), :]
            bv = b_ref[pl.ds(base, SCAN_ROWS), :]
            hs = [None] * nt
            order = range(nt) if dirn == 0 else range(nt - 1, -1, -1)
            for k in order:
                a, b = _tile_scan(av[k * SUBLANES:(k + 1) * SUBLANES], bv[k * SUBLANES:(k + 1) * SUBLANES],
                                  dirn == 1)
                h = a * carry + b
                carry = h[SUBLANES - 1:SUBLANES] if dirn == 0 else h[0:1]
                hs[k] = h
            if emit is not None:
                emit(base, jnp.concatenate(hs, axis=0))
            return carry

        carry_ref[...] = lax.fori_loop(0, ng, body, carry_ref[...])

    def run(dirn):
        first = (j == 0) if dirn == 0 else (j == nc)
        ci = j if dirn == 0 else 2 * nc - 1 - j

        @pl.when(first)
        def _():
            zeros = jnp.zeros((SUBLANES, xe_ref.shape[1]), F32)
            xe_ref[0:SUBLANES] = zeros
            xe_ref[SUBLANES:SUBLANES + c_len] = xac_ref[0]
            xe_ref[SUBLANES + c_len:2 * SUBLANES + c_len] = zeros
            gates(conv(c_len), c_len, dirn)
            carry_ref[...] = jnp.zeros(carry_ref.shape, F32)
            scan(c_len, dirn, None)

        r0 = pl.multiple_of(ci * tc, tc)
        if dirn == 0:
            p0 = pl.multiple_of(jnp.maximum(r0 - SUBLANES, 0), SUBLANES)
            n0 = pl.multiple_of(jnp.minimum(r0 + tc, t_len - SUBLANES), SUBLANES)
            prev = xa_ref[0, pl.ds(p0, SUBLANES), :]
            nxt = xa_ref[0, pl.ds(n0, SUBLANES), :]
            xe_ref[0:SUBLANES] = jnp.where(ci > 0, prev, 0.0)
            xe_ref[SUBLANES:SUBLANES + tc] = xa_ref[0, pl.ds(r0, tc), :]
            xe_ref[SUBLANES + tc:2 * SUBLANES + tc] = jnp.where(ci < nc - 1, nxt, 0.0)
            u = conv(tc)
            u_ref[pl.ds(r0, tc), :] = u
        else:
            u = u_ref[pl.ds(r0, tc), :]
        gates(u, tc, dirn)

        if dirn == 0:
            def emit(base, h):
                hf_ref[pl.ds(r0 + base, SCAN_ROWS), :] = h
        else:
            def emit(base, h):
                tot = hf_ref[pl.ds(r0 + base, SCAN_ROWS), :] + h
                ga = ga_ref[0, pl.ds(base, SCAN_ROWS), :]
                y_ref[0, pl.ds(base, SCAN_ROWS), :] = (tot * jax.nn.gelu(ga)).astype(y_ref.dtype)
        scan(tc, dirn, emit)

    @pl.when(j < nc)
    def _():
        run(0)

    @pl.when(j >= nc)
    def _():
        run(1)


def _lru_call(xg, xg_c, conv_w, conv_b, wg, bg, lam):
    b, t, _ = xg.shape
    c_len = xg_c.shape[1]
    d_lru = conv_w.shape[1]
    tc = min(t, LRU_CHUNK)
    nc = t // tc
    assert c_len <= tc and c_len % SCAN_ROWS == 0 and tc % SCAN_ROWS == 0

    def chunk_map(bi, j):
        return (bi, jnp.where(j < nc, nc - 1, 2 * nc - 1 - j), 0)

    def ga_map(bi, j):
        return (bi, jnp.where(j < nc, nc - 1, 2 * nc - 1 - j), 1)

    return pl.pallas_call(
        functools.partial(_lru_kernel, t_len=t, c_len=c_len, tc=tc, nc=nc),
        grid=(b, 2 * nc),
        in_specs=[
            pl.BlockSpec((1, t, d_lru), lambda bi, j: (bi, 0, 0)),
            pl.BlockSpec((1, tc, d_lru), ga_map),
            pl.BlockSpec((1, c_len, d_lru), lambda bi, j: (bi, 0, 0)),
            pl.BlockSpec(conv_w.shape, lambda bi, j: (0, 0)),
            pl.BlockSpec((1, d_lru), lambda bi, j: (0, 0)),
            pl.BlockSpec(wg.shape, lambda bi, j: (0, 0, 0, 0)),
            pl.BlockSpec(bg.shape, lambda bi, j: (0, 0, 0, 0)),
            pl.BlockSpec(lam.shape, lambda bi, j: (0, 0)),
        ],
        out_specs=pl.BlockSpec((1, tc, d_lru), chunk_map),
        out_shape=jax.ShapeDtypeStruct((b, t, d_lru), BF16),
        scratch_shapes=[
            pltpu.VMEM((t, d_lru), F32),
            pltpu.VMEM((t, d_lru), F32),
            pltpu.VMEM((tc + 2 * SUBLANES, d_lru), F32),
            pltpu.VMEM((tc, d_lru), F32),
            pltpu.VMEM((tc, d_lru), F32),
            pltpu.VMEM((1, d_lru), F32),
        ],
        compiler_params=_cparams("parallel", "arbitrary"),
        name="lru",
    )(xg, xg, xg_c, conv_w, conv_b.reshape(1, d_lru), wg, bg, lam)


def _ret_kernel(q_ref, k_ref, v_ref, g_ref, kc_ref, vc_ref, cos_ref, sin_ref, dm_ref, rt_ref, cwt_ref, gc_ref,
                y_ref, qr_ref, kr_ref, sf_ref, sb_ref, *, nc, scale):
    C = RET_CHUNK
    dh = q_ref.shape[2]
    def rope(xv, cs, sn):
        return xv * cs + pltpu.roll(xv, dh // 2, 1) * sn

    xi_f, xi_b, ze_f, ze_b = rt_ref[0, 0], rt_ref[0, 1], rt_ref[0, 2], rt_ref[0, 3]
    g_chunk = gc_ref[0]
    g_full = jnp.concatenate([g_chunk] * (dh // SUBLANES), axis=0)

    kc = (kc_ref[0] * scale).T.astype(BF16)
    vc = vc_ref[0]
    s_f0 = jnp.dot(kc, (vc * cwt_ref[0, 0]).astype(BF16), preferred_element_type=F32)
    s_b0 = jnp.dot(kc, (vc * cwt_ref[0, 1]).astype(BF16), preferred_element_type=F32)

    def p1(n, s):
        r = pl.multiple_of(n * C, C)
        cs, sn = cos_ref[pl.ds(r, C), :], sin_ref[pl.ds(r, C), :]
        k = rope(k_ref[0, pl.ds(r, C), :], cs, sn) * scale
        qr_ref[pl.ds(r, C), :] = rope(q_ref[0, pl.ds(r, C), :], cs, sn)
        kr_ref[pl.ds(r, C), :] = k.astype(BF16)
        v = v_ref[0, pl.ds(r, C), :]
        vz = jnp.concatenate([v * ze_f, v * ze_b], axis=1).astype(BF16)
        kv = jnp.dot(k.T.astype(BF16), vz, preferred_element_type=F32)
        sf_ref[n] = s
        sb_ref[n] = kv[:, dh:]
        return g_full * s + kv[:, :dh]

    lax.fori_loop(0, nc, p1, s_f0, unroll=min(RET_UNROLL, nc))

    def p1b(i, s):
        n = nc - 1 - i
        kvb = sb_ref[n]
        sb_ref[n] = s
        return g_full * s + kvb

    lax.fori_loop(0, nc, p1b, s_b0)

    def p2(n, carry):
        r = pl.multiple_of(n * C, C)
        q = qr_ref[pl.ds(r, C), :]
        kb = kr_ref[pl.ds(r, C), :]
        vb = v_ref[0, pl.ds(r, C), :].astype(BF16)
        sc = lax.dot_general(q.astype(BF16), kb, NT_DIMS, preferred_element_type=F32) * dm_ref[0]
        o = jnp.dot(sc.astype(BF16), vb, preferred_element_type=F32)
        qx = jnp.concatenate([q * xi_f, q * xi_b], axis=1).astype(BF16)
        st = jnp.concatenate([sf_ref[n], sb_ref[n]], axis=0).astype(BF16)
        o = o + jnp.dot(qx, st, preferred_element_type=F32)
        o = o * lax.rsqrt(jnp.mean(o * o, axis=-1, keepdims=True) + EPS)
        y_ref[0, pl.ds(r, C), :] = (o * _silu(g_ref[0, pl.ds(r, C), :])).astype(y_ref.dtype)
        return carry

    lax.fori_loop(0, nc, p2, 0, unroll=min(RET_UNROLL, nc))


def _ret_tables(t, c_len, dh):
    C = RET_CHUNK
    hh = jnp.arange(RET_HEADS, dtype=F32)
    log_g = jnp.log1p(-jnp.exp2(-(5.0 + 2.0 * hh)))
    idx = jnp.arange(C, dtype=F32)
    dm = jnp.exp(jnp.abs(idx[:, None] - idx[None, :])[None] * log_g[:, None, None])

    def lanes(e):
        return jnp.broadcast_to(jnp.exp(e[None, :] * log_g[:, None])[:, :, None], (RET_HEADS, e.shape[0], dh))

    rt = jnp.stack([lanes(idx + 1.0), lanes(C - idx), lanes(C - 1.0 - idx), lanes(idx)], axis=1)
    pos = jnp.arange(c_len, dtype=F32)
    cwt = jnp.stack([lanes(c_len - 1.0 - pos), lanes(pos)], axis=1)
    gc = jnp.broadcast_to(jnp.exp(C * log_g)[:, None, None], (RET_HEADS, SUBLANES, dh))
    rows = t // GRID_W
    row = jnp.repeat(jnp.arange(rows, dtype=F32), GRID_W)
    col = jnp.tile(jnp.arange(GRID_W, dtype=F32), rows)
    q4 = dh // 4
    freqs = ROPE_BASE ** (-jnp.arange(q4, dtype=F32) / q4)
    ang_r = row[:, None] * freqs[None, :]
    ang_c = col[:, None] * freqs[None, :]
    cos = jnp.concatenate([jnp.cos(ang_r), jnp.cos(ang_c)] * 2, axis=1)
    sin = jnp.concatenate([-jnp.sin(ang_r), -jnp.sin(ang_c), jnp.sin(ang_r), jnp.sin(ang_c)], axis=1)
    return dm, rt, cwt, gc, cos, sin


def _pair_apart(w_in, n_xg):
    d, n_all = w_in.shape
    dh = (n_all - n_xg) // (4 * RET_HEADS)
    qk = w_in[:, n_xg:n_xg + 2 * RET_HEADS * dh].reshape(d, 2 * RET_HEADS, 2, 2, dh // 4)
    qk = jnp.swapaxes(qk, 2, 3).reshape(d, 2 * RET_HEADS * dh)
    return jnp.concatenate([w_in[:, :n_xg], qk, w_in[:, n_xg + 2 * RET_HEADS * dh:]], axis=1)


def _ret_call(qkvg, qkvg_c):
    b, t, w = qkvg.shape
    c_len = qkvg_c.shape[1]
    dh = w // (4 * RET_HEADS)
    nc = t // RET_CHUNK
    dm, rt, cwt, gc, cos, sin = _ret_tables(t, c_len, dh)
    H = RET_HEADS

    def col(off):
        return lambda bi, h: (bi, 0, off + h)

    return pl.pallas_call(
        functools.partial(_ret_kernel, nc=nc, scale=dh ** -0.5),
        grid=(b, H),
        in_specs=[
            pl.BlockSpec((1, t, dh), col(0)),
            pl.BlockSpec((1, t, dh), col(H)),
            pl.BlockSpec((1, t, dh), col(2 * H)),
            pl.BlockSpec((1, t, dh), col(3 * H)),
            pl.BlockSpec((1, c_len, dh), col(H)),
            pl.BlockSpec((1, c_len, dh), col(2 * H)),
            pl.BlockSpec((t, dh), lambda bi, h: (0, 0)),
            pl.BlockSpec((t, dh), lambda bi, h: (0, 0)),
            pl.BlockSpec((1, RET_CHUNK, RET_CHUNK), lambda bi, h: (h, 0, 0)),
            pl.BlockSpec((1, 4, RET_CHUNK, dh), lambda bi, h: (h, 0, 0, 0)),
            pl.BlockSpec((1, 2, c_len, dh), lambda bi, h: (h, 0, 0, 0)),
            pl.BlockSpec((1, SUBLANES, dh), lambda bi, h: (h, 0, 0)),
        ],
        out_specs=pl.BlockSpec((1, t, dh), col(0)),
        out_shape=jax.ShapeDtypeStruct((b, t, H * dh), BF16),
        scratch_shapes=[
            pltpu.VMEM((t, dh), F32),
            pltpu.VMEM((t, dh), BF16),
            pltpu.VMEM((nc, dh, dh), F32),
            pltpu.VMEM((nc, dh, dh), F32),
        ],
        compiler_params=_cparams("parallel", "arbitrary"),
        name="retention",
    )(qkvg, qkvg, qkvg, qkvg, qkvg_c, qkvg_c, cos, sin, dm, rt, cwt, gc)


def _route(logits_t, bias):
    e, n = logits_t.shape
    per = e // N_GROUPS
    s = jax.nn.sigmoid(logits_t)
    sb = s + bias
    neg = -jnp.inf
    g3 = sb.reshape(N_GROUPS, per, n)
    it3 = lax.broadcasted_iota(jnp.int32, g3.shape, 1)
    m1 = jnp.max(g3, axis=1, keepdims=True)
    i1 = jnp.min(jnp.where(g3 == m1, it3, per), axis=1, keepdims=True)
    m2 = jnp.max(jnp.where(it3 == i1, neg, g3), axis=1, keepdims=True)
    gs = (m1 + m2).reshape(N_GROUPS, n)
    itg = lax.broadcasted_iota(jnp.int32, gs.shape, 0)
    gsel = jnp.zeros(gs.shape, jnp.bool_)
    for _ in range(TOP_GROUPS):
        m = jnp.max(gs, axis=0, keepdims=True)
        i = jnp.min(jnp.where(gs == m, itg, N_GROUPS), axis=0, keepdims=True)
        hit = itg == i
        gsel = gsel | hit
        gs = jnp.where(hit, neg, gs)
    gmask = jnp.broadcast_to(gsel.reshape(N_GROUPS, 1, n), (N_GROUPS, per, n)).reshape(e, n)
    v = jnp.where(gmask, sb, neg)
    ite = lax.broadcasted_iota(jnp.int32, v.shape, 0)
    sel = jnp.zeros(v.shape, jnp.bool_)
    for _ in range(TOP_K):
        m = jnp.max(v, axis=0, keepdims=True)
        i = jnp.min(jnp.where(v == m, ite, e), axis=0, keepdims=True)
        hit = ite == i
        sel = sel | hit
        v = jnp.where(hit, neg, v)
    picked = jnp.where(sel, s, 0.0)
    return picked / jnp.sum(picked, axis=0, keepdims=True) * ROUTE_SCALE


def _outproj_kernel(yl_ref, yr_ref, x_ref, g1_ref, sh_ref, sc_ref, nf_ref, wo_ref, wr_ref, rb_ref,
                    x1_ref, h2_ref, gt_ref, cnt_ref):
    dl = yl_ref.shape[2]
    ts = gt_ref.shape[1] // cnt_ref.shape[0]
    w = wr_ref[...]
    w_hi = w.astype(BF16)
    r1 = w - w_hi.astype(F32)
    w_mid = r1.astype(BF16)
    w_lo = (r1 - w_mid.astype(F32)).astype(BF16)
    for k in range(cnt_ref.shape[0]):
        rows = pl.ds(k * ts, ts)
        mx = jnp.dot(yl_ref[0, rows, :], wo_ref[0:dl, :], preferred_element_type=F32)
        mx = mx + jnp.dot(yr_ref[0, rows, :], wo_ref[dl:, :], preferred_element_type=F32)
        x1 = x_ref[0, rows, :] + g1_ref[0] * mx
        x1_ref[0, rows, :] = x1
        h = _rms_mod(x1, nf_ref[...], sh_ref[0], sc_ref[0])
        h_hi = h.astype(BF16)
        h2_ref[0, rows, :] = h_hi
        h_lo = (h - h_hi.astype(F32)).astype(BF16)
        small = (_dot32(h_hi, w_lo) + _dot32(h_lo, w_mid)) + (_dot32(h_hi, w_mid) + _dot32(h_lo, w_hi))
        logits = small + _dot32(h_hi, w_hi)
        ne = logits.shape[1]
        logits_t = jnp.concatenate([logits, jnp.zeros((ts, LANES - ne), F32)], axis=1).T[:ne, :]
        gates = _route(logits_t, rb_ref[...])
        gt_ref[:, rows] = gates
        hit = (gates > 0.0).astype(BF16)
        cnt_ref[k] = lax.dot_general(jnp.ones((SUBLANES, ts), BF16), hit, NT_DIMS, preferred_element_type=F32)


def _outproj_call(y_lru, y_ret, x, g1, sh2, sc2, norm_ffn, w_out_bf, w_router, b_router):
    b, t, d = x.shape
    dl = y_lru.shape[2]
    e = w_router.shape[1]
    ts = min(t, MOE_TILE)
    tt = min(t, OUTPROJ_TILES * ts)
    nt = t // tt
    per = tt // ts
    mod_map = lambda bi, i: (bi, 0, 0)
    return pl.pallas_call(
        _outproj_kernel,
        grid=(b, nt),
        in_specs=[
            pl.BlockSpec((1, tt, dl), lambda bi, i: (bi, i, 0)),
            pl.BlockSpec((1, tt, d - dl), lambda bi, i: (bi, i, 0)),
            pl.BlockSpec((1, tt, d), lambda bi, i: (bi, i, 0)),
            pl.BlockSpec((1, 1, d), mod_map),
            pl.BlockSpec((1, 1, d), mod_map),
            pl.BlockSpec((1, 1, d), mod_map),
            pl.BlockSpec((1, d), lambda bi, i: (0, 0)),
            pl.BlockSpec((d, d), lambda bi, i: (0, 0)),
            pl.BlockSpec((d, e), lambda bi, i: (0, 0)),
            pl.BlockSpec((e, 1), lambda bi, i: (0, 0)),
        ],
        out_specs=[
            pl.BlockSpec((1, tt, d), lambda bi, i: (bi, i, 0)),
            pl.BlockSpec((1, tt, d), lambda bi, i: (bi, i, 0)),
            pl.BlockSpec((e, tt), lambda bi, i: (0, bi * nt + i)),
            pl.BlockSpec((per, SUBLANES, e), lambda bi, i: (bi * nt + i, 0, 0)),
        ],
        out_shape=[
            jax.ShapeDtypeStruct((b, t, d), F32),
            jax.ShapeDtypeStruct((b, t, d), BF16),
            jax.ShapeDtypeStruct((e, b * t), F32),
            jax.ShapeDtypeStruct((b * nt * per, SUBLANES, e), F32),
        ],
        compiler_params=_cparams("parallel", "arbitrary"),
        name="outproj_route",
    )(y_lru, y_ret, x, g1, sh2, sc2, norm_ffn.reshape(1, d), w_out_bf, w_router, b_router.reshape(e, 1))


MOE_TILE = 256
MOE_CHUNK = 16
MOE_RB = 512
MOE_ROWS_COMMON = 2048
MOE_BM = 2048
MOE_EXPERT_ROWS = 1024
OUTPROJ_TILES = 2
NO_RANK = 512.0

TN_DIMS = (((0,), (0,)), ((), ()))
TAB_LOFF, TAB_GBASE, TAB_CPAD, TAB_NROWS, TAB_EEND = range(5)


def _round_up(v, m):
    return -(-v // m) * m


def _tables_kernel(cnt_ref, loff_ref, gbase_ref, cpad_ref, eend_ref, blk_ref, nused_ref, ctab_ref, *, nb):
    cnt = cnt_ref[...]
    ns, ne = cnt.shape
    cpad = jnp.ceil(cnt * (1.0 / MOE_CHUNK)) * MOE_CHUNK
    upper = (lax.broadcasted_iota(jnp.int32, (ne, ne), 0) < lax.broadcasted_iota(jnp.int32, (ne, ne), 1)).astype(F32)
    lower = (lax.broadcasted_iota(jnp.int32, (ns, ns), 1) < lax.broadcasted_iota(jnp.int32, (ns, ns), 0)).astype(F32)
    loff = jnp.dot(cpad, upper, precision=HIGHEST, preferred_element_type=F32)
    before = jnp.dot(lower, cpad, precision=HIGHEST, preferred_element_type=F32)
    tot = jnp.sum(cpad, axis=0, keepdims=True)
    epad = jnp.ceil(tot * (1.0 / MOE_BM)) * MOE_BM
    epad8 = jnp.broadcast_to(epad, (SUBLANES, ne))
    ebase = jnp.dot(epad8, upper, precision=HIGHEST, preferred_element_type=F32)[0:1]
    eend = ebase + epad
    loff_ref[...] = loff
    gbase_ref[...] = ebase + before
    cpad_ref[...] = cpad
    eend_ref[...] = eend
    start = lax.broadcasted_iota(jnp.int32, (nb, ne), 0).astype(F32) * MOE_BM
    blk = jnp.sum((eend <= start).astype(F32), axis=1, keepdims=True)
    blk_ref[...] = jnp.minimum(blk, ne - 1.0)
    nused_ref[...] = jnp.sum(epad, axis=1, keepdims=True) * (1.0 / MOE_BM)
    nch = ctab_ref.shape[1]
    cidx = lax.broadcasted_iota(jnp.int32, (ns, nch), 1).astype(F32)
    shift = (ebase + before - loff) * (1.0 / MOE_CHUNK)
    ctab = jnp.full((ns, nch), -1.0, F32)
    for e in range(ne):
        lo = loff[:, e:e + 1] * (1.0 / MOE_CHUNK)
        hi = lo + cpad[:, e:e + 1] * (1.0 / MOE_CHUNK)
        ctab = jnp.where((cidx >= lo) & (cidx < hi), cidx + shift[:, e:e + 1], ctab)
    ctab_ref[...] = ctab


def _tables_call(cnt, nb, nch):
    ns, ne = cnt.shape
    f = lambda shape: jax.ShapeDtypeStruct(shape, F32)
    return pl.pallas_call(
        functools.partial(_tables_kernel, nb=nb),
        out_shape=[f((ns, ne)), f((ns, ne)), f((ns, ne)), f((1, ne)), f((nb, 1)), f((1, 1)), f((ns, nch))],
        compiler_params=pltpu.CompilerParams(vmem_limit_bytes=VMEM_LIMIT_BYTES),
        name="moe_tables",
    )(cnt)


def _tile_rank_rhs(gt, lcol):
    ne, ts = gt.shape
    hit = gt > 0.0
    before = (lax.broadcasted_iota(jnp.int32, (ts, ts), 0) < lax.broadcasted_iota(jnp.int32, (ts, ts), 1)).astype(BF16)
    rank = jnp.dot(hit.astype(BF16), before, preferred_element_type=F32)
    rm = jnp.where(hit, rank, NO_RANK)
    q = jnp.broadcast_to(lcol * (1.0 / MOE_CHUNK), (ne, LANES))
    return jnp.concatenate([rm, q], axis=1).astype(BF16)


def _onehot_rows(r0, rhs, lrow, crow, ts):
    ne = lrow.shape[1]
    rows = (r0 + lax.broadcasted_iota(jnp.int32, (MOE_RB, ne), 0)).astype(F32)
    seg = ((rows >= lrow) & (rows < crow)).astype(BF16)
    ex = jnp.dot(seg, rhs, preferred_element_type=F32)
    rowl = (r0 + lax.broadcasted_iota(jnp.int32, (MOE_RB, LANES), 0)).astype(F32)
    tgt = rowl - MOE_CHUNK * ex[:, ts:ts + LANES]
    p = ex[:, :ts] == jnp.concatenate([tgt] * (ts // LANES), axis=1)
    return p, (ex[:, ts + LANES:] if rhs.shape[1] > ts + LANES else None)


def _chunk_copy(buf, slot, c, row, hbm, sem, to_hbm):
    loc = buf.at[slot, pl.ds(c * MOE_CHUNK, MOE_CHUNK)]
    glob = hbm.at[pl.ds(pl.multiple_of(row, MOE_CHUNK), MOE_CHUNK)]
    return pltpu.make_async_copy(loc, glob, sem.at[slot]) if to_hbm else pltpu.make_async_copy(glob, loc, sem.at[slot])


def _wait_chunks(n, buf, slot, hbm, sem, to_hbm):
    for _ in range(n):
        _chunk_copy(buf, slot, 0, 0, hbm, sem, to_hbm).wait()


def _wait_chunks_dyn(n, buf, slot, hbm, sem, to_hbm):
    def body(i, c):
        _chunk_copy(buf, slot, 0, 0, hbm, sem, to_hbm).wait()
        return c

    lax.fori_loop(0, n, body, 0)


def _dispatch_kernel(tab_ref, ctab_ref, h_ref, gt_ref, lrow_ref, crow_ref, lcol_ref, xs_hbm, xbuf, zbuf, sem, tailf,
                     *, ns, trash0):
    s = pl.program_id(0)
    slot = s % 2
    ts = h_ref.shape[0]
    ne = gt_ref.shape[0]
    rows_max = xbuf.shape[1]
    nch = rows_max // MOE_CHUNK
    rows_c = min(rows_max, MOE_ROWS_COMMON)
    nch_c = rows_c // MOE_CHUNK

    def wait_slot(sl):
        _wait_chunks(nch_c, xbuf, sl, xs_hbm, sem, True)
        if nch > nch_c:
            @pl.when(tailf[sl] == 1)
            def _():
                _wait_chunks(nch - nch_c, xbuf, sl, xs_hbm, sem, True)

    @pl.when(s >= 2)
    def _():
        wait_slot(slot)

    rhs = _tile_rank_rhs(gt_ref[...], lcol_ref[0])

    def rows_block(r0):
        p, _ = _onehot_rows(r0, rhs, lrow_ref[0], crow_ref[0], ts)
        xbuf[slot, r0:r0 + MOE_RB, :] = jnp.dot(p.astype(BF16), h_ref[...],
                                                preferred_element_type=F32).astype(xbuf.dtype)

    def copy_chunks(c0, c1):
        for c in range(c0, c1):
            g = ctab_ref[0, 0, c]
            row = jnp.where(g >= 0, g * MOE_CHUNK, trash0 + slot * rows_max + c * MOE_CHUNK)
            _chunk_copy(xbuf, slot, c, row, xs_hbm, sem, True).start(priority=c % 2)

    for r0 in range(0, rows_c, MOE_RB):
        rows_block(r0)
    copy_chunks(0, nch_c)
    if nch > nch_c:
        long_tile = ctab_ref[0, 0, nch_c] >= 0
        tailf[slot] = long_tile.astype(jnp.int32)

        @pl.when(long_tile)
        def _():
            for r0 in range(rows_c, rows_max, MOE_RB):
                rows_block(r0)
            copy_chunks(nch_c, nch)

    @pl.when(s == ns - 1)
    def _():
        zbuf[...] = jnp.zeros(zbuf.shape, zbuf.dtype)

        def per_expert(e, total):
            first = tab_ref[0, TAB_GBASE, e] + tab_ref[0, TAB_CPAD, e]
            ntail = lax.shift_right_logical(tab_ref[0, TAB_EEND, e] - first, MOE_CHUNK.bit_length() - 1)

            def per_chunk(j, c2):
                dst = xs_hbm.at[pl.ds(pl.multiple_of(first + j * MOE_CHUNK, MOE_CHUNK), MOE_CHUNK)]
                pltpu.make_async_copy(zbuf, dst, sem.at[slot]).start()
                return c2

            lax.fori_loop(0, ntail, per_chunk, 0)
            return total + ntail

        ntail = lax.fori_loop(0, ne, per_expert, 0)
        _wait_chunks_dyn(ntail, xbuf, slot, xs_hbm, sem, True)
        wait_slot(slot)
        if ns >= 2:
            wait_slot(1 - slot)


def _dispatch_call(tab, ctab, h2, gt, lrow, crow, lcol, p_max, rows_max):
    n, d = h2.shape
    ne = gt.shape[0]
    ts = min(n, MOE_TILE)
    ns = n // ts
    nch = rows_max // MOE_CHUNK
    return pl.pallas_call(
        functools.partial(_dispatch_kernel, ns=ns, trash0=p_max),
        grid=(ns,),
        in_specs=[
            pl.BlockSpec((1, 5, ne), lambda s: (s, 0, 0), memory_space=pltpu.SMEM),
            pl.BlockSpec((1, 1, nch), lambda s: (s, 0, 0), memory_space=pltpu.SMEM),
            pl.BlockSpec((ts, d), lambda s: (s, 0)),
            pl.BlockSpec((ne, ts), lambda s: (0, s)),
            pl.BlockSpec((1, 1, ne), lambda s: (s, 0, 0)),
            pl.BlockSpec((1, 1, ne), lambda s: (s, 0, 0)),
            pl.BlockSpec((1, ne, 1), lambda s: (s, 0, 0)),
        ],
        out_specs=pl.BlockSpec(memory_space=pl.ANY),
        out_shape=jax.ShapeDtypeStruct((p_max + 2 * rows_max, d), BF16),
        scratch_shapes=[
            pltpu.VMEM((2, rows_max, d), BF16),
            pltpu.VMEM((MOE_CHUNK, d), BF16),
            pltpu.SemaphoreType.DMA((2,)),
            pltpu.SMEM((2,), jnp.int32),
        ],
        compiler_params=_cparams("arbitrary"),
        name="moe_dispatch",
    )(tab, ctab, h2, gt, lrow, crow, lcol)


def _experts_kernel(blk_ref, nused_ref, x_ref, w1_ref, w3_ref, w2_ref, y_ref):
    del blk_ref

    @pl.when(pl.program_id(0) < nused_ref[0])
    def _():
        for r0 in range(0, x_ref.shape[0], MOE_EXPERT_ROWS):
            xv = x_ref[r0:r0 + MOE_EXPERT_ROWS, :]
            hid = _silu(_dot32(xv, w1_ref[0])) * _dot32(xv, w3_ref[0])
            y_ref[r0:r0 + MOE_EXPERT_ROWS, :] = _dot32(hid.astype(BF16), w2_ref[0]).astype(y_ref.dtype)


def _experts_call(blk, nused, xs, w1, w3, w2):
    d = xs.shape[1]
    _, _, de = w1.shape
    nb = blk.shape[0]
    p_max = nb * MOE_BM
    row_map = lambda i, blk_r, nu_r: (jnp.maximum(jnp.minimum(i, nu_r[0] - 1), 0), 0)
    w_map = lambda i, blk_r, nu_r: (blk_r[i], 0, 0)
    return pl.pallas_call(
        _experts_kernel,
        grid_spec=pltpu.PrefetchScalarGridSpec(
            num_scalar_prefetch=2,
            grid=(nb,),
            in_specs=[
                pl.BlockSpec((MOE_BM, d), row_map),
                pl.BlockSpec((1, d, de), w_map),
                pl.BlockSpec((1, d, de), w_map),
                pl.BlockSpec((1, de, d), w_map),
            ],
            out_specs=pl.BlockSpec((MOE_BM, d), row_map),
        ),
        out_shape=jax.ShapeDtypeStruct((p_max, d), BF16),
        compiler_params=_cparams("arbitrary"),
        name="moe_experts",
    )(blk, nused, xs, w1, w3, w2)


def _combine_kernel(ctab_ref, ctabn_ref, gt_ref, lrow_ref, crow_ref, lcol_ref, h_ref, x1_ref, g2_ref,
                    s1_ref, s3_ref, s2_ref, nfin_ref, ys_hbm, o_ref, ybuf, acc_ref, sem, *, ns):
    s = pl.program_id(0)
    slot = s % 2
    ts = h_ref.shape[0]
    rows_max = ybuf.shape[1]
    nch = rows_max // MOE_CHUNK
    rows_c = min(rows_max, MOE_ROWS_COMMON)
    nch_c = rows_c // MOE_CHUNK

    def chunk_range(tab, sl, c0, c1):
        for c in range(c0, c1):
            row = jnp.maximum(tab[0, 0, c], 0) * MOE_CHUNK
            _chunk_copy(ybuf, sl, c, row, ys_hbm, sem, False).start(priority=c % 2)

    def gather(tab, sl):
        chunk_range(tab, sl, 0, nch_c)
        if nch > nch_c:
            @pl.when(tab[0, 0, nch_c] >= 0)
            def _():
                chunk_range(tab, sl, nch_c, nch)

    def wait(tab, sl):
        _wait_chunks(nch_c, ybuf, sl, ys_hbm, sem, False)
        if nch > nch_c:
            @pl.when(tab[0, 0, nch_c] >= 0)
            def _():
                _wait_chunks(nch - nch_c, ybuf, sl, ys_hbm, sem, False)

    @pl.when(s == 0)
    def _():
        gather(ctab_ref, 0)

    gather(ctabn_ref, 1 - slot)

    h = h_ref[...]
    hs = _silu(jnp.dot(h, s1_ref[...], preferred_element_type=F32)) * jnp.dot(h, s3_ref[...], preferred_element_type=F32)
    acc = jnp.dot(hs.astype(BF16), s2_ref[...], preferred_element_type=F32)

    gt = gt_ref[...]
    rhs = jnp.concatenate([_tile_rank_rhs(gt, lcol_ref[0]), gt.astype(BF16)], axis=1)
    wait(ctab_ref, slot)

    def rows_block(r0):
        p, gexp = _onehot_rows(r0, rhs, lrow_ref[0], crow_ref[0], ts)
        gm = jnp.where(p, gexp, 0.0).astype(BF16)
        return lax.dot_general(gm, ybuf[slot, r0:r0 + MOE_RB, :], TN_DIMS, preferred_element_type=F32)

    for r0 in range(0, rows_c, MOE_RB):
        acc = acc + rows_block(r0)
    if nch > nch_c:
        acc_ref[...] = acc

        @pl.when(ctab_ref[0, 0, nch_c] >= 0)
        def _():
            for r0 in range(rows_c, rows_max, MOE_RB):
                acc_ref[...] += rows_block(r0)

        acc = acc_ref[...]
    xo = x1_ref[...] + g2_ref[0] * acc
    o_ref[...] = xo * lax.rsqrt(jnp.mean(xo * xo, axis=-1, keepdims=True) + EPS) * nfin_ref[...]

    @pl.when(s == ns - 1)
    def _():
        wait(ctabn_ref, 1 - slot)


def _combine_call(ctab, gt, lrow, crow, lcol, h2, x1, g2, ws1, ws3, ws2, norm_final, ys, rows_max, t):
    n, d = h2.shape
    ne = gt.shape[0]
    ts = min(n, MOE_TILE)
    ns = n // ts
    per_b = t // ts
    nch = rows_max // MOE_CHUNK
    return pl.pallas_call(
        functools.partial(_combine_kernel, ns=ns),
        grid=(ns,),
        in_specs=[
            pl.BlockSpec((1, 1, nch), lambda s: (s, 0, 0), memory_space=pltpu.SMEM),
            pl.BlockSpec((1, 1, nch), lambda s: (jnp.minimum(s + 1, ns - 1), 0, 0), memory_space=pltpu.SMEM),
            pl.BlockSpec((ne, ts), lambda s: (0, s)),
            pl.BlockSpec((1, 1, ne), lambda s: (s, 0, 0)),
            pl.BlockSpec((1, 1, ne), lambda s: (s, 0, 0)),
            pl.BlockSpec((1, ne, 1), lambda s: (s, 0, 0)),
            pl.BlockSpec((ts, d), lambda s: (s, 0)),
            pl.BlockSpec((ts, d), lambda s: (s, 0)),
            pl.BlockSpec((1, 1, d), lambda s: (s // per_b, 0, 0)),
            pl.BlockSpec(ws1.shape, lambda s: (0, 0)),
            pl.BlockSpec(ws3.shape, lambda s: (0, 0)),
            pl.BlockSpec(ws2.shape, lambda s: (0, 0)),
            pl.BlockSpec((1, d), lambda s: (0, 0)),
            pl.BlockSpec(memory_space=pl.ANY),
        ],
        out_specs=pl.BlockSpec((ts, d), lambda s: (s, 0)),
        out_shape=jax.ShapeDtypeStruct((n, d), F32),
        scratch_shapes=[
            pltpu.VMEM((2, rows_max, d), BF16),
            pltpu.VMEM((ts, d), F32),
            pltpu.SemaphoreType.DMA((2,)),
        ],
        compiler_params=_cparams("arbitrary"),
        name="moe_combine",
    )(ctab, ctab, gt, lrow, crow, lcol, h2, x1, g2, ws1, ws3, ws2, norm_final.reshape(1, d), ys)


def _moe_call(h2, gt, cnt, x1, g2, w1, w3, w2, ws1, ws3, ws2, norm_final, t):
    n, d = h2.shape
    ne = gt.shape[0]
    ts = min(n, MOE_TILE)
    ns = n // ts
    rows_max = _round_up(TOP_K * ts + ne * (MOE_CHUNK - 1), MOE_RB)
    p_max = _round_up(n * TOP_K + ns * ne * (MOE_CHUNK - 1) + ne * (MOE_BM - 1), MOE_BM)
    nb = p_max // MOE_BM

    nch = rows_max // MOE_CHUNK
    loff, gbase, cpad, eend, blk, nused, ctab = _tables_call(cnt[:, 0, :], nb, nch)
    nrows = loff[:, ne - 1:] + cpad[:, ne - 1:]
    tab = jnp.stack([loff, gbase, cpad, jnp.broadcast_to(nrows, (ns, ne)), jnp.broadcast_to(eend, (ns, ne))],
                    axis=1).astype(jnp.int32)
    ctab = ctab.astype(jnp.int32).reshape(ns, 1, nch)
    lrow = loff.reshape(ns, 1, ne)
    crow = (loff + cpad).reshape(ns, 1, ne)
    lcol = loff.reshape(ns, ne, 1)

    xs = _dispatch_call(tab, ctab, h2, gt, lrow, crow, lcol, p_max, rows_max)
    ys = _experts_call(blk.reshape(nb).astype(jnp.int32), nused.reshape(1).astype(jnp.int32), xs, w1, w3, w2)
    return _combine_call(ctab, gt, lrow, crow, lcol, h2, x1, g2, ws1, ws3, ws2, norm_final, ys, rows_max, t)


def _block_diag(w):
    nb, n, _ = w.shape
    eye = jnp.eye(nb, dtype=w.dtype)
    return (eye[:, None, :, None] * w[:, :, None, :]).reshape(nb * n, nb * n)


def _lru_gate_weights(w_rec, b_rec, w_inp, b_inp):
    d_lru = b_rec.shape[1]
    half = d_lru // 2
    ws, bs = [], []
    for dirn in range(2):
        wr, wi = _block_diag(w_rec[dirn]), _block_diag(w_inp[dirn])
        ws.append(jnp.stack([jnp.concatenate([wr[s:s + half, s:s + half], wi[s:s + half, s:s + half]], axis=1)
                             for s in (0, half)]))
        bs.append(jnp.stack([jnp.concatenate([b_rec[dirn, s:s + half], b_inp[dirn, s:s + half]])[None, :]
                             for s in (0, half)]))
    return jnp.stack(ws).astype(BF16), jnp.stack(bs).astype(F32)


def kernel(x, c, ctx, c_ctx, w_mod, b_mod, norm_mix, norm_ffn, w_in, conv_w, conv_b, lru_w_rec, lru_b_rec, lru_w_in, lru_b_in, lru_lambda, w_out, w_router, b_router, w_exp_gate, w_exp_up, w_exp_down, w_sh_gate, w_sh_up, w_sh_down, norm_final):
    b, t, d = x.shape
    depth = w_mod.shape[0]
    assert depth == 1, "context-stream update for deeper stacks is not implemented"
    l = 0
    d_lru = conv_w.shape[2]
    n_xg = 2 * d_lru

    rows = -(-(b + 1) // SUBLANES) * SUBLANES
    cc = jnp.concatenate([c, c_ctx[None, :], jnp.zeros((rows - b - 1, d), F32)], axis=0)
    mod = _mod_call(cc, w_mod[l], b_mod[l]).reshape(rows, 1, -1)
    sh1, sc1, g1, sh2, sc2, g2 = (mod[:, :, i * d:(i + 1) * d] for i in range(6))

    w_in_bf = _pair_apart(w_in[l], n_xg).astype(BF16)
    gain_mix = norm_mix[l].reshape(1, d)
    xg, qkvg = _inproj_call(x, sh1[:b], sc1[:b], gain_mix, w_in_bf, n_xg, True)
    xg_c, qkvg_c = _inproj_call(ctx, sh1[b:b + 1], sc1[b:b + 1], gain_mix, w_in_bf, n_xg, False)

    wg, bg = _lru_gate_weights(lru_w_rec[l], lru_b_rec[l], lru_w_in[l], lru_b_in[l])
    y_lru = _lru_call(xg, xg_c, conv_w[l], conv_b[l], wg, bg, lru_lambda[l])
    y_ret = _ret_call(qkvg, qkvg_c)

    x1, h2, gt, cnt = _outproj_call(y_lru, y_ret, x, g1[:b], sh2[:b], sc2[:b], norm_ffn[l], w_out[l].astype(BF16),
                                    w_router[l], b_router[l])

    out = _moe_call(h2.reshape(b * t, d), gt, cnt, x1.reshape(b * t, d), g2[:b],
                    w_exp_gate[l].astype(BF16), w_exp_up[l].astype(BF16), w_exp_down[l].astype(BF16),
                    w_sh_gate[l].astype(BF16), w_sh_up[l].astype(BF16), w_sh_down[l].astype(BF16), norm_final, t)
    return out.reshape(b, t, d)
```
